```python
import math
import jax, jax.numpy as jnp
from jax import lax
import numpy as np

D_MODEL = 1024
BATCH = 4
SEQ = 4096
DEPTH = 2

CTX_LEN = 256
GRID_W = 64
Q_BLOCK = 128
ROPE_THETA = 10000.0
EPS = 1e-6

MLA_HEADS = 4
MLA_Q_RANK = 384
MLA_KV_RANK = 256
MLA_NOPE = 64
MLA_ROPE = 32
MLA_V = 64
NA_HEADS = 4
NA_DIM = 64
NA_ROWS = 8
NA_COLS = 16
DIFF_HEADS = 4
DIFF_QK = 32
DIFF_V = 64
GQA_HEADS = 4
GQA_KV_HEADS = 2
GQA_DIM = 64
N_BRANCH = 4
BRANCH_W = 256
D_FF = -(-8 * D_MODEL // (3 * 256)) * 256
DEEPNORM_ALPHA = (2 * DEPTH) ** 0.25
DEEPNORM_BETA = (8 * DEPTH) ** -0.25

IN_SIZES = (
    MLA_Q_RANK, MLA_KV_RANK, MLA_ROPE,
    NA_HEADS * NA_DIM, NA_HEADS * NA_DIM, NA_HEADS * NA_DIM,
    DIFF_HEADS * 2 * DIFF_QK, DIFF_HEADS * 2 * DIFF_QK, DIFF_HEADS * DIFF_V,
    GQA_HEADS * GQA_DIM, GQA_KV_HEADS * GQA_DIM, GQA_KV_HEADS * GQA_DIM,
    N_BRANCH * D_MODEL,
)
D_IN = sum(IN_SIZES)

kernel_name = 'hybrid_parallel_mixer_dit_block'


def layer_norm(x, g, b):
    xf = x.astype(jnp.float32)
    mu = jnp.mean(xf, axis=-1, keepdims=True)
    var = jnp.mean(jnp.square(xf - mu), axis=-1, keepdims=True)
    return ((xf - mu) * lax.rsqrt(var + EPS) * g + b).astype(x.dtype)


def rms_norm(x, g):
    xf = x.astype(jnp.float32)
    y = xf * lax.rsqrt(jnp.mean(jnp.square(xf), axis=-1, keepdims=True) + EPS)
    return (y * g).astype(x.dtype)


def rope_tables(S, rot_dim, dtype):
    t = jnp.arange(S)
    pos = jnp.stack([t // GRID_W, t % GRID_W], axis=-1).astype(jnp.float32)
    n_f = rot_dim // 4
    inv = ROPE_THETA ** (-jnp.arange(n_f, dtype=jnp.float32) / n_f)
    ang = pos[:, :, None] * inv
    return jnp.cos(ang).astype(dtype), jnp.sin(ang).astype(dtype)


def apply_rope(x, tab):
    cos, sin = tab
    shp = x.shape
    n_f = shp[-1] // 4
    xr = x.reshape(shp[:-1] + (2, 2, n_f))
    x1, x2 = xr[..., 0, :], xr[..., 1, :]
    bshape = (1, shp[1]) + (1,) * (x.ndim - 3) + (2, n_f)
    c, s = cos.reshape(bshape), sin.reshape(bshape)
    return jnp.stack([x1 * c - x2 * s, x1 * s + x2 * c], axis=-2).reshape(shp)


def attn_probs(q, k, scale):
    B, Q, H, d = q.shape
    Hk = k.shape[2]
    qg = q.reshape(B, Q, Hk, H // Hk, d)
    s = jnp.einsum('bqkgd,btkd->bkgqt', qg, k).astype(jnp.float32) * scale
    return jax.nn.softmax(s, axis=-1)


def attn_apply(p, v):
    o = jnp.einsum('bkgqt,btkd->bqkgd', p.astype(v.dtype), v)
    B, Q, Hk, G, dv = o.shape
    return o.reshape(B, Q, Hk * G, dv)


def dense_attention(q, k, v, scale):
    return attn_apply(attn_probs(q, k, scale), v)


def diff_attention(q, k, v, lam, g_sub, lam_init):
    scale = DIFF_QK ** -0.5
    p1 = attn_probs(q[:, :, :, 0], k[:, :, :, 0], scale)
    p2 = attn_probs(q[:, :, :, 1], k[:, :, :, 1], scale)
    p = p1 - lam[:, None, None, None] * p2
    return rms_norm(attn_apply(p, v), g_sub) * (1.0 - lam_init)


def sweep_query_blocks(fn, q):
    B, S = q.shape[:2]
    nb = S // Q_BLOCK
    qb = jnp.moveaxis(q.reshape((B, nb, Q_BLOCK) + q.shape[2:]), 1, 0)
    o = lax.map(fn, qb)
    return jnp.moveaxis(o, 0, 1).reshape((B, S) + o.shape[3:])


def neighbourhood_attention(q, k, v, k_ctx, v_ctx, rpb):
    B, S, H, d = q.shape
    rows = S // GRID_W
    kh, kw = min(NA_ROWS, rows), NA_COLS
    scale = d ** -0.5
    col = np.arange(GRID_W)
    cs = np.clip(col - kw // 2, 0, GRID_W - kw)
    col_idx = cs[:, None] + np.arange(kw)[None, :]
    col_off = col_idx - col[:, None] + (NA_COLS - 1)
    kg = k.reshape(B, rows, GRID_W, H, d)
    vg = v.reshape(B, rows, GRID_W, H, d)
    qr = jnp.moveaxis(q.reshape(B, rows, GRID_W, H, d), 1, 0)

    def row_block(args):
        r, q_row = args
        rs = jnp.clip(r - kh // 2, 0, rows - kh)
        kb = lax.dynamic_slice_in_dim(kg, rs, kh, axis=1)[:, :, col_idx]
        vb = lax.dynamic_slice_in_dim(vg, rs, kh, axis=1)[:, :, col_idx]
        row_off = rs + jnp.arange(kh) - r + (NA_ROWS - 1)
        bias = jnp.transpose(rpb[:, row_off][:, :, col_off], (0, 2, 1, 3))
        s_loc = jnp.einsum('bwhd,biwjhd->bhwij', q_row, kb).astype(jnp.float32) * scale + bias[None].astype(jnp.float32)
        s_ctx = jnp.einsum('bwhd,bchd->bhwc', q_row, k_ctx).astype(jnp.float32) * scale
        s = jnp.concatenate([s_loc.reshape(B, H, GRID_W, kh * kw), s_ctx], axis=-1)
        p = jax.nn.softmax(s, axis=-1).astype(v.dtype)
        p_loc = p[..., :kh * kw].reshape(B, H, GRID_W, kh, kw)
        p_ctx = p[..., kh * kw:]
        return jnp.einsum('bhwij,biwjhd->bwhd', p_loc, vb) + jnp.einsum('bhwc,bchd->bwhd', p_ctx, v_ctx)

    o = lax.map(row_block, (jnp.arange(rows), qr))
    return jnp.moveaxis(o, 0, 1).reshape(B, S, H, d)


def mixer_inputs(h, w_in, g_q_a, w_q_up, g_kv_a, w_kv_up, g_qn, g_kn, tabs):
    B, T, _ = h.shape
    split_at = [int(i) for i in np.cumsum(IN_SIZES)[:-1]]
    (cq, ckv, kpe, q_na, k_na, v_na, q_df, k_df, v_df, q_gq, k_gq, v_gq, gates) = jnp.split(h @ w_in, split_at, axis=-1)
    qa = (rms_norm(cq, g_q_a) @ w_q_up).reshape(B, T, MLA_HEADS, MLA_NOPE + MLA_ROPE)
    kva = (rms_norm(ckv, g_kv_a) @ w_kv_up).reshape(B, T, MLA_HEADS, MLA_NOPE + MLA_V)
    q_nope, q_pe = qa[..., :MLA_NOPE], qa[..., MLA_NOPE:]
    k_nope, va = kva[..., :MLA_NOPE], kva[..., MLA_NOPE:]
    kpe = kpe.reshape(B, T, 1, MLA_ROPE)
    q_df = q_df.reshape(B, T, DIFF_HEADS, 2, DIFF_QK)
    k_df = k_df.reshape(B, T, DIFF_HEADS, 2, DIFF_QK)
    q_gq = rms_norm(q_gq.reshape(B, T, GQA_HEADS, GQA_DIM), g_qn)
    k_gq = rms_norm(k_gq.reshape(B, T, GQA_KV_HEADS, GQA_DIM), g_kn)
    if tabs is not None:
        q_pe = apply_rope(q_pe, tabs[MLA_ROPE])
        kpe = apply_rope(kpe, tabs[MLA_ROPE])
        q_df = apply_rope(q_df, tabs[DIFF_QK])
        k_df = apply_rope(k_df, tabs[DIFF_QK])
        q_gq = apply_rope(q_gq, tabs[GQA_DIM])
        k_gq = apply_rope(k_gq, tabs[GQA_DIM])
    qa = jnp.concatenate([q_nope, q_pe], axis=-1)
    ka = jnp.concatenate([k_nope, jnp.broadcast_to(kpe, (B, T, MLA_HEADS, MLA_ROPE))], axis=-1)
    return (qa, ka, va,
            q_na.reshape(B, T, NA_HEADS, NA_DIM), k_na.reshape(B, T, NA_HEADS, NA_DIM), v_na.reshape(B, T, NA_HEADS, NA_DIM),
            q_df, k_df, v_df.reshape(B, T, DIFF_HEADS, DIFF_V),
            q_gq, k_gq, v_gq.reshape(B, T, GQA_KV_HEADS, GQA_DIM),
            gates)


def merge_branches(gates, ys, w_branch, w_out):
    B, T, _ = gates.shape
    g = jax.nn.sigmoid(gates.reshape(B, T, N_BRANCH, D_MODEL))
    acc = g[:, :, 0] * (ys[0].reshape(B, T, BRANCH_W) @ w_branch[0])
    for i in range(1, N_BRANCH):
        acc = acc + g[:, :, i] * (ys[i].reshape(B, T, BRANCH_W) @ w_branch[i])
    return acc @ w_out


def token_mixers(hl, hc, tabs, w_in, g_q_a, w_q_up, g_kv_a, w_kv_up, rpb, lam_q1, lam_k1, lam_q2, lam_k2,
                 g_sub, g_qn, g_kn, w_branch, w_out, lam_init, need_ctx):
    proj = (w_in, g_q_a, w_q_up, g_kv_a, w_kv_up, g_qn, g_kn)
    (qa, ka, va, qb, kb, vb, qc, kc, vc, qd, kd, vd, gl) = mixer_inputs(hl, *proj, tabs)
    (xqa, xka, xva, xqb, xkb, xvb, xqc, xkc, xvc, xqd, xkd, xvd, gx) = mixer_inputs(hc, *proj, None)
    lam = (jnp.exp(jnp.sum(lam_q1 * lam_k1, axis=-1).astype(jnp.float32))
           - jnp.exp(jnp.sum(lam_q2 * lam_k2, axis=-1).astype(jnp.float32)) + lam_init)
    cat = lambda a, b: jnp.concatenate([a, b], axis=1)
    ka_all, va_all = cat(ka, xka), cat(va, xva)
    kc_all, vc_all = cat(kc, xkc), cat(vc, xvc)
    kd_all, vd_all = cat(kd, xkd), cat(vd, xvd)
    mla_scale = (MLA_NOPE + MLA_ROPE) ** -0.5
    ya = sweep_query_blocks(lambda q: dense_attention(q, ka_all, va_all, mla_scale), qa)
    yb = neighbourhood_attention(qb, kb, vb, xkb, xvb, rpb)
    yc = sweep_query_blocks(lambda q: diff_attention(q, kc_all, vc_all, lam, g_sub, lam_init), qc)
    yd = sweep_query_blocks(lambda q: dense_attention(q, kd_all, vd_all, GQA_DIM ** -0.5), qd)
    out_l = merge_branches(gl, (ya, yb, yc, yd), w_branch, w_out)
    if not need_ctx:
        return out_l, None
    ya_c = dense_attention(xqa, xka, xva, mla_scale)
    yb_c = dense_attention(xqb, xkb, xvb, NA_DIM ** -0.5)
    yc_c = diff_attention(xqc, xkc, xvc, lam, g_sub, lam_init)
    yd_c = dense_attention(xqd, xkd, xvd, GQA_DIM ** -0.5)
    out_c = merge_branches(gx, (ya_c, yb_c, yc_c, yd_c), w_branch, w_out)
    return out_l, out_c


def swiglu(h, w_gate_up, w_down):
    g, u = jnp.split(h @ w_gate_up, 2, axis=-1)
    return (jax.nn.silu(g) * u) @ w_down


def setup_inputs(seed: int = 0) -> dict:
    key = jax.random.key(seed)
    ks = jax.random.split(key, 27)
    f32 = jnp.float32
    nrm = lambda k, shape, s: jax.random.normal(k, shape, f32) * s
    gain = lambda k, shape: 1.0 + 0.01 * jax.random.normal(k, shape, f32)
    L = DEPTH
    return {
        'x': nrm(ks[0], (BATCH, SEQ, D_MODEL), 1.0),
        'c': nrm(ks[1], (BATCH, D_MODEL), 1.0),
        'ctx': nrm(ks[2], (BATCH, CTX_LEN, D_MODEL), 1.0),
        'c_ctx': nrm(ks[3], (D_MODEL,), 1.0),
        'w_ada': nrm(ks[4], (L, D_MODEL, 6 * D_MODEL), 0.5 * D_MODEL ** -0.5),
        'b_ada': nrm(ks[5], (L, 6 * D_MODEL), 0.01),
        'w_in': nrm(ks[6], (L, D_MODEL, D_IN), D_MODEL ** -0.5),
        'g_q_a': gain(ks[7], (L, MLA_Q_RANK)),
        'w_q_up': nrm(ks[8], (L, MLA_Q_RANK, MLA_HEADS * (MLA_NOPE + MLA_ROPE)), MLA_Q_RANK ** -0.5),
        'g_kv_a': gain(ks[9], (L, MLA_KV_RANK)),
        'w_kv_up': nrm(ks[10], (L, MLA_KV_RANK, MLA_HEADS * (MLA_NOPE + MLA_V)), MLA_KV_RANK ** -0.5),
        'rpb': nrm(ks[11], (L, NA_HEADS, 2 * NA_ROWS - 1, 2 * NA_COLS - 1), 0.1),
        'lam_q1': nrm(ks[12], (L, DIFF_HEADS, DIFF_QK), 0.1),
        'lam_k1': nrm(ks[13], (L, DIFF_HEADS, DIFF_QK), 0.1),
        'lam_q2': nrm(ks[14], (L, DIFF_HEADS, DIFF_QK), 0.1),
        'lam_k2': nrm(ks[15], (L, DIFF_HEADS, DIFF_QK), 0.1),
        'g_sub': gain(ks[16], (L, DIFF_V)),
        'g_qn': gain(ks[17], (L, GQA_DIM)),
        'g_kn': gain(ks[18], (L, GQA_DIM)),
        'w_branch': nrm(ks[19], (L, N_BRANCH, BRANCH_W, D_MODEL), BRANCH_W ** -0.5),
        'w_out': nrm(ks[20], (L, D_MODEL, D_MODEL), DEEPNORM_BETA * D_MODEL ** -0.5),
        'ln1_g': gain(ks[21], (L, D_MODEL)),
        'ln1_b': nrm(ks[22], (L, D_MODEL), 0.01),
        'w_gate_up': nrm(ks[23], (L, D_MODEL, 2 * D_FF), D_MODEL ** -0.5),
        'w_down': nrm(ks[24], (L, D_FF, D_MODEL), DEEPNORM_BETA * D_FF ** -0.5),
        'ln2_g': gain(ks[25], (L, D_MODEL)),
        'ln2_b': nrm(ks[26], (L, D_MODEL), 0.01),
    }


def reference(x, c, ctx, c_ctx, w_ada, b_ada, w_in, g_q_a, w_q_up, g_kv_a, w_kv_up, rpb,
              lam_q1, lam_k1, lam_q2, lam_k2, g_sub, g_qn, g_kn, w_branch, w_out,
              ln1_g, ln1_b, w_gate_up, w_down, ln2_g, ln2_b):
    S = x.shape[1]
    tabs = {d: rope_tables(S, d, x.dtype) for d in (MLA_ROPE, DIFF_QK, GQA_DIM)}
    s_lat = jax.nn.silu(c)
    s_ctx = jax.nn.silu(c_ctx)
    xl, xc = x, ctx
    for l in range(DEPTH):
        need_ctx = l < DEPTH - 1
        lam_init = 0.8 - 0.6 * math.exp(-0.3 * l)
        mod_l = (s_lat @ w_ada[l] + b_ada[l])[:, None, :]
        mod_c = (s_ctx @ w_ada[l] + b_ada[l])[None, None, :]
        sh1, sc1, g1, sh2, sc2, g2 = jnp.split(mod_l, 6, axis=-1)
        xsh1, xsc1, xg1, xsh2, xsc2, xg2 = jnp.split(mod_c, 6, axis=-1)
        hl = xl * (1.0 + sc1) + sh1
        hc = xc * (1.0 + xsc1) + xsh1
        out_l, out_c = token_mixers(hl, hc, tabs, w_in[l], g_q_a[l], w_q_up[l], g_kv_a[l], w_kv_up[l], rpb[l],
                                    lam_q1[l], lam_k1[l], lam_q2[l], lam_k2[l], g_sub[l], g_qn[l], g_kn[l],
                                    w_branch[l], w_out[l], lam_init, need_ctx)
        xl = layer_norm(DEEPNORM_ALPHA * xl + g1 * out_l, ln1_g[l], ln1_b[l])
        hl = xl * (1.0 + sc2) + sh2
        xl = layer_norm(DEEPNORM_ALPHA * xl + g2 * swiglu(hl, w_gate_up[l], w_down[l]), ln2_g[l], ln2_b[l])
        if need_ctx:
            xc = layer_norm(DEEPNORM_ALPHA * xc + xg1 * out_c, ln1_g[l], ln1_b[l])
            hc = xc * (1.0 + xsc2) + xsh2
            xc = layer_norm(DEEPNORM_ALPHA * xc + xg2 * swiglu(hc, w_gate_up[l], w_down[l]), ln2_g[l], ln2_b[l])
    return xl
```

```python
import functools
import math

import jax
import jax.numpy as jnp
import numpy as np
from jax import lax
from jax.experimental import pallas as pl
from jax.experimental.pallas import tpu as pltpu

F32 = jnp.float32
BF16 = jnp.bfloat16

D_MODEL = 1024
DEPTH = 2
CTX_LEN = 256
GRID_W = 64
ROPE_THETA = 10000.0
EPS = 1e-6

MLA_HEADS = 4
MLA_Q_RANK = 384
MLA_KV_RANK = 256
MLA_NOPE = 64
MLA_ROPE = 32
MLA_V = 64
NA_HEADS = 4
NA_DIM = 64
NA_ROWS = 8
NA_COLS = 16
DIFF_HEADS = 4
DIFF_QK = 32
DIFF_V = 64
GQA_HEADS = 4
GQA_KV_HEADS = 2
GQA_DIM = 64
N_BRANCH = 4
BRANCH_W = 256
D_FF = -(-8 * D_MODEL // (3 * 256)) * 256
DEEPNORM_ALPHA = (2 * DEPTH) ** 0.25

IN_SIZES = (
    MLA_Q_RANK, MLA_KV_RANK, MLA_ROPE,
    NA_HEADS * NA_DIM, NA_HEADS * NA_DIM, NA_HEADS * NA_DIM,
    DIFF_HEADS * 2 * DIFF_QK, DIFF_HEADS * 2 * DIFF_QK, DIFF_HEADS * DIFF_V,
    GQA_HEADS * GQA_DIM, GQA_KV_HEADS * GQA_DIM, GQA_KV_HEADS * GQA_DIM,
    N_BRANCH * D_MODEL,
)
IN_OFFS = tuple(int(v) for v in np.concatenate([[0], np.cumsum(IN_SIZES)]))
GATE_OFF = IN_OFFS[12]

LANES = 128
HEAD_PAD = LANES
V_PAD = 2 * LANES
TOK_TILE = 256
Q_TILE = 512
NA_GROUP_ROWS = 4
NA_WIN_ROWS = 12
MASK_VALUE = -1e30
VMEM_LIMIT = 56 * 1024 * 1024

MLA_SCALE = (MLA_NOPE + MLA_ROPE) ** -0.5
NA_SCALE = NA_DIM ** -0.5
DIFF_SCALE = DIFF_QK ** -0.5
GQA_SCALE = GQA_DIM ** -0.5

P_CQ = 0
P_CKV = P_CQ + MLA_Q_RANK
P_KPE = P_CKV + MLA_KV_RANK
P_QN = P_KPE + HEAD_PAD
P_KN = P_QN + NA_HEADS * HEAD_PAD
P_VN = P_KN + NA_HEADS * HEAD_PAD
P_QD = P_VN + NA_HEADS * HEAD_PAD
P_KD = P_QD + DIFF_HEADS * HEAD_PAD
P_VD = P_KD + DIFF_HEADS * HEAD_PAD
P_QG = P_VD + DIFF_HEADS * HEAD_PAD
P_KG = P_QG + GQA_HEADS * HEAD_PAD
P_VG = P_KG + GQA_KV_HEADS * HEAD_PAD
P_END = P_VG + GQA_KV_HEADS * HEAD_PAD

NT_DIMS = (((1,), (1,)), ((), ()))


def _cparams(sem):
    return pltpu.CompilerParams(dimension_semantics=sem, vmem_limit_bytes=VMEM_LIMIT)


def _const_spec(shape):
    nd = len(shape)
    return pl.BlockSpec(shape, lambda *_: (0,) * nd, pipeline_mode=pl.Buffered(1))


def _head_cols(base, n_heads, dup):
    cols = []
    for h in range(n_heads):
        c = list(range(base + 64 * h, base + 64 * h + 64))
        cols += c + (c if dup else [-1] * 64)
    return cols


def _small_cols():
    o = IN_OFFS
    cols = list(range(o[0], o[1])) + list(range(o[1], o[2]))
    cols += [-1] * 64 + list(range(o[2], o[3])) + [-1] * 32
    cols += _head_cols(o[3], NA_HEADS, False) + _head_cols(o[4], NA_HEADS, False) + _head_cols(o[5], NA_HEADS, True)
    cols += _head_cols(o[6], DIFF_HEADS, False) + _head_cols(o[7], DIFF_HEADS, False) + _head_cols(o[8], DIFF_HEADS, True)
    cols += _head_cols(o[9], GQA_HEADS, False) + _head_cols(o[10], GQA_KV_HEADS, False) + _head_cols(o[11], GQA_KV_HEADS, True)
    assert len(cols) == P_END
    return np.asarray(cols, np.int32)


def _gather_cols(w, idx):
    valid = jnp.asarray(idx >= 0)
    return jnp.where(valid[None, :], jnp.take(w, jnp.asarray(np.maximum(idx, 0)), axis=1), 0.0)


def _qup_cols():
    cols = []
    for h in range(MLA_HEADS):
        cols += list(range(96 * h, 96 * h + 96)) + [-1] * 32
    return np.asarray(cols, np.int32)


def _kvup_cols():
    kc, vc = [], []
    for h in range(MLA_HEADS):
        kc += list(range(128 * h, 128 * h + 64)) + [-1] * 64
        v = list(range(128 * h + 64, 128 * h + 128))
        vc += v + v
    return np.asarray(kc, np.int32), np.asarray(vc, np.int32)


def _rope_pattern(S, rot_dim):
    t = jnp.arange(S)
    pos = jnp.stack([t // GRID_W, t % GRID_W], axis=-1).astype(F32)
    n_f = rot_dim // 4
    inv = ROPE_THETA ** (-jnp.arange(n_f, dtype=F32) / n_f)
    ang = pos[:, :, None] * inv
    cos, sin = jnp.cos(ang), jnp.sin(ang)
    c = jnp.concatenate([cos[:, 0], cos[:, 0], cos[:, 1], cos[:, 1]], axis=-1)
    s = jnp.concatenate([-sin[:, 0], sin[:, 0], -sin[:, 1], sin[:, 1]], axis=-1)
    c = jnp.concatenate([c, jnp.ones((CTX_LEN, rot_dim), F32)], axis=0)
    s = jnp.concatenate([s, jnp.zeros((CTX_LEN, rot_dim), F32)], axis=0)
    return c, s


def _rope_tables(S):
    T = S + CTX_LEN
    c32, s32 = _rope_pattern(S, 32)
    c64, s64 = _rope_pattern(S, 64)
    one = lambda n: jnp.ones((T, n), F32)
    zero = lambda n: jnp.zeros((T, n), F32)
    cat = lambda xs: jnp.concatenate(xs, axis=-1)
    ca, sa = cat([one(64), c32, one(32)]), cat([zero(64), s32, zero(32)])
    cd, sd = cat([c32, c32, one(64)]), cat([s32, s32, zero(64)])
    cg, sg = cat([c64, one(64)]), cat([s64, zero(64)])
    return ca, sa, cd, sd, cg, sg


def _na_bias_tables(rpb_l, rows):
    n_groups = rows // NA_GROUP_ROWS
    kh, kw = min(NA_ROWS, rows), NA_COLS
    a = np.arange(NA_GROUP_ROWS)[:, None, None, None]
    w = np.arange(GRID_W)[None, :, None, None]
    i = np.arange(NA_WIN_ROWS)[None, None, :, None]
    c = np.arange(GRID_W)[None, None, None, :]
    out = []
    for g in (0, 1, n_groups - 1):
        r = NA_GROUP_ROWS * g + a
        win = np.clip(NA_GROUP_ROWS * g - kh // 2, 0, rows - NA_WIN_ROWS)
        i_abs = win + i
        rs = np.clip(r - kh // 2, 0, rows - kh)
        cs = np.clip(w - kw // 2, 0, GRID_W - kw)
        valid = (i_abs >= rs) & (i_abs < rs + kh) & (c >= cs) & (c < cs + kw)
        row_off = np.clip(i_abs - r + (NA_ROWS - 1), 0, 2 * NA_ROWS - 2)
        col_off = np.clip(c - w + (NA_COLS - 1), 0, 2 * NA_COLS - 2)
        shape = (NA_GROUP_ROWS, GRID_W, NA_WIN_ROWS, GRID_W)
        valid = np.broadcast_to(valid, shape).reshape(NA_GROUP_ROWS * GRID_W, NA_WIN_ROWS * GRID_W)
        row_off = np.broadcast_to(row_off, shape).reshape(valid.shape)
        col_off = np.broadcast_to(col_off, shape).reshape(valid.shape)
        b = rpb_l[:, jnp.asarray(row_off), jnp.asarray(col_off)]
        out.append(jnp.where(jnp.asarray(valid)[None], b.astype(F32), MASK_VALUE))
    return jnp.stack(out, axis=0)


def _sigmoid(x):
    return 1.0 / (1.0 + jnp.exp(-x))


def _rms(x, g, n):
    ms = jnp.sum(x * x, axis=-1, keepdims=True) * (1.0 / n)
    return x * lax.rsqrt(ms + EPS) * g


def _layer_norm(x, g, b):
    mu = jnp.mean(x, axis=-1, keepdims=True)
    xc = x - mu
    var = jnp.mean(xc * xc, axis=-1, keepdims=True)
    return xc * lax.rsqrt(var + EPS) * g + b


def _rope(z, c, s, n_f):
    lane = lax.broadcasted_iota(jnp.int32, z.shape, 1)
    first = (lane & n_f) == 0
    up = pltpu.roll(z, LANES - n_f, 1)
    dn = pltpu.roll(z, n_f, 1)
    return z * c + jnp.where(first, up, dn) * s


def _bdot(a, b):
    return jnp.dot(a, b, preferred_element_type=F32)


def _ada_kernel(c_ref, w_ref, b_ref, o_ref):
    c = c_ref[...]
    a = c * _sigmoid(c)
    o_ref[0] = jnp.dot(a, w_ref[0], preferred_element_type=F32, precision=lax.Precision.HIGHEST) + b_ref[0]


def _ada_call(c8, w_ada, b_ada):
    L, D, N = w_ada.shape
    tn = 1536
    return pl.pallas_call(
        _ada_kernel,
        grid=(L, N // tn),
        in_specs=[
            pl.BlockSpec((8, D), lambda l, n: (0, 0)),
            pl.BlockSpec((1, D, tn), lambda l, n: (l, 0, n)),
            pl.BlockSpec((1, 1, tn), lambda l, n: (l, 0, n)),
        ],
        out_specs=pl.BlockSpec((1, 8, tn), lambda l, n: (l, 0, n)),
        out_shape=jax.ShapeDtypeStruct((L, 8, N), F32),
        compiler_params=_cparams(("parallel", "parallel")),
        name="ada_mod",
    )(c8, w_ada, b_ada.reshape(L, 1, N))


def _proj_kernel(x_ref, ctx_ref, modl_ref, modc_ref, w_ref, gqa_ref, wqup_ref, gkva_ref, wkvk_ref, wkvv_ref,
                 gqn_ref, gkn_ref, ca_ref, sa_ref, cd_ref, sd_ref, cg_ref, sg_ref,
                 qa_ref, ka_ref, va_ref, qn_ref, kn_ref, vn_ref, qd_ref, kd_ref, vd_ref, qg_ref, kg_ref, vg_ref,
                 *, n_lat_tiles):
    D = D_MODEL
    is_ctx = pl.program_id(0) == n_lat_tiles
    x = jnp.where(is_ctx, ctx_ref[0], x_ref[0])
    ml, mc = modl_ref[0], modc_ref[0]
    sh = jnp.where(is_ctx, mc[:, 0:D], ml[:, 0:D])
    sc = jnp.where(is_ctx, mc[:, D:2 * D], ml[:, D:2 * D])
    h = (x * (1.0 + sc) + sh).astype(BF16)
    z = _bdot(h, w_ref[...])
    tm = z.shape[0]
    ones = jnp.ones((tm, LANES), F32)
    lane = lax.broadcasted_iota(jnp.int32, (tm, LANES), 1)
    head = lambda base, i: z[:, base + HEAD_PAD * i: base + HEAD_PAD * (i + 1)]
    with_ones = lambda v: jnp.concatenate([v, ones], axis=-1).astype(BF16)

    ca, sa = ca_ref[...], sa_ref[...]
    cqn = _rms(z[:, P_CQ:P_CKV], gqa_ref[...], MLA_Q_RANK).astype(BF16)
    qa_full = _bdot(cqn, wqup_ref[...])
    ckvn = _rms(z[:, P_CKV:P_KPE], gkva_ref[...], MLA_KV_RANK).astype(BF16)
    k_full = _bdot(ckvn, wkvk_ref[...])
    v_full = _bdot(ckvn, wkvv_ref[...])
    kpe = _rope(z[:, P_KPE:P_QN], ca, sa, MLA_ROPE // 4)
    for i in range(MLA_HEADS):
        sl = slice(HEAD_PAD * i, HEAD_PAD * (i + 1))
        qa_ref[0, i] = (_rope(qa_full[:, sl], ca, sa, MLA_ROPE // 4) * MLA_SCALE).astype(BF16)
        ka_ref[0, i] = (k_full[:, sl] + kpe).astype(BF16)
        va_ref[0, i] = with_ones(v_full[:, sl])

    for i in range(NA_HEADS):
        qn_ref[0, i] = (head(P_QN, i) * NA_SCALE).astype(BF16)
        kn_ref[0, i] = head(P_KN, i).astype(BF16)
        vn_ref[0, i] = with_ones(head(P_VN, i))

    cd, sd = cd_ref[...], sd_ref[...]
    for i in range(DIFF_HEADS):
        zq = _rope(head(P_QD, i), cd, sd, DIFF_QK // 4) * DIFF_SCALE
        qd_ref[0, 2 * i] = jnp.where(lane < DIFF_QK, zq, 0.0).astype(BF16)
        qd_ref[0, 2 * i + 1] = jnp.where(lane >= DIFF_QK, zq, 0.0).astype(BF16)
        kd_ref[0, i] = _rope(head(P_KD, i), cd, sd, DIFF_QK // 4).astype(BF16)
        vd_ref[0, i] = with_ones(head(P_VD, i))

    cg, sg = cg_ref[...], sg_ref[...]
    gqn, gkn = gqn_ref[...], gkn_ref[...]
    for i in range(GQA_HEADS):
        qn = _rms(head(P_QG, i), gqn, GQA_DIM)
        qg_ref[0, i] = (_rope(qn, cg, sg, GQA_DIM // 4) * GQA_SCALE).astype(BF16)
    for i in range(GQA_KV_HEADS):
        kn = _rms(head(P_KG, i), gkn, GQA_DIM)
        kg_ref[0, i] = _rope(kn, cg, sg, GQA_DIM // 4).astype(BF16)
        vg_ref[0, i] = with_ones(head(P_VG, i))


def _proj_call(xl, xc, mod_l, mod_c, wsm, gqa, wqup, gkva, wkvk, wkvv, gqn, gkn, tabs):
    B, S, D = xl.shape
    tm = TOK_TILE
    n_lat = S // tm
    T = S + CTX_LEN
    grid = (n_lat + 1, B)
    tab_spec = pl.BlockSpec((tm, LANES), lambda t, b: (t, 0))
    in_specs = [
        pl.BlockSpec((1, tm, D), lambda t, b: (b, jnp.minimum(t, n_lat - 1), 0)),
        pl.BlockSpec((1, tm, D), lambda t, b: (b, 0, 0)),
        pl.BlockSpec((1, 1, 6 * D), lambda t, b: (b, 0, 0)),
        pl.BlockSpec((1, 1, 6 * D), lambda t, b: (0, 0, 0)),
        _const_spec(wsm.shape), _const_spec(gqa.shape), _const_spec(wqup.shape), _const_spec(gkva.shape),
        _const_spec(wkvk.shape), _const_spec(wkvv.shape), _const_spec(gqn.shape), _const_spec(gkn.shape),
    ] + [tab_spec] * 6

    def o(n_heads, width):
        return (jax.ShapeDtypeStruct((B, n_heads, T, width), BF16),
                pl.BlockSpec((1, n_heads, tm, width), lambda t, b: (b, 0, t, 0)))

    outs = [o(MLA_HEADS, HEAD_PAD), o(MLA_HEADS, HEAD_PAD), o(MLA_HEADS, V_PAD),
            o(NA_HEADS, HEAD_PAD), o(NA_HEADS, HEAD_PAD), o(NA_HEADS, V_PAD),
            o(2 * DIFF_HEADS, HEAD_PAD), o(DIFF_HEADS, HEAD_PAD), o(DIFF_HEADS, V_PAD),
            o(GQA_HEADS, HEAD_PAD), o(GQA_KV_HEADS, HEAD_PAD), o(GQA_KV_HEADS, V_PAD)]
    return pl.pallas_call(
        functools.partial(_proj_kernel, n_lat_tiles=n_lat),
        grid=grid,
        in_specs=in_specs,
        out_specs=[s for _, s in outs],
        out_shape=[s for s, _ in outs],
        compiler_params=_cparams(("arbitrary", "arbitrary")),
        name="in_proj",
    )(xl, xc, mod_l, mod_c, wsm, gqa, wqup, gkva, wkvk, wkvv, gqn, gkn, *tabs)


def _softmax_pv(q, k, v):
    s = lax.dot_general(q, k, NT_DIMS, preferred_element_type=F32)
    m = jnp.max(s, axis=-1, keepdims=True)
    p = jnp.exp(s - m).astype(BF16)
    oe = _bdot(p, v)
    return oe[:, :LANES] / oe[:, LANES:]


def _attn_kernel(*refs, n_kv, group, diff, lam_init):
    if diff:
        q_ref, k_ref, v_ref, lq1_ref, lk1_ref, lq2_ref, lk2_ref, gsub_ref, o_ref = refs
    else:
        q_ref, k_ref, v_ref, o_ref = refs
    tq = q_ref.shape[2]
    heads = []
    for j in range(n_kv):
        q = q_ref[0, j * group:(j + 1) * group].reshape(group * tq, HEAD_PAD)
        on = _softmax_pv(q, k_ref[0, j], v_ref[0, j])
        heads += [on[g * tq:(g + 1) * tq] for g in range(group)]
    if diff:
        ys = []
        for j in range(2):
            dot_l = lambda a, b: jnp.sum(a[0, j:j + 1] * b[0, j:j + 1], axis=-1, keepdims=True)
            lam = jnp.exp(dot_l(lq1_ref, lk1_ref)) - jnp.exp(dot_l(lq2_ref, lk2_ref)) + lam_init
            y = heads[2 * j] - lam * heads[2 * j + 1]
            ys.append(_rms(y, gsub_ref[...], LANES) * (1.0 - lam_init))
        heads = ys
    lane = lax.broadcasted_iota(jnp.int32, (tq, LANES), 1)
    o_ref[0] = jnp.where(lane < 64, heads[0], heads[1]).astype(BF16)


def _attn_call(q, k, v, *, group, n_kv, tq, q_blk0, n_q, t_blk, k_blk0, diff_args=None, lam_init=0.0, name):
    B, Hq = q.shape[0], q.shape[1]
    n_pairs = Hq // (n_kv * group)
    qh = n_kv * group
    in_specs = [
        pl.BlockSpec((1, qh, tq, HEAD_PAD), lambda b, p, i: (b, p, q_blk0 + i, 0)),
        pl.BlockSpec((1, n_kv, t_blk, HEAD_PAD), lambda b, p, i: (b, p, k_blk0, 0)),
        pl.BlockSpec((1, n_kv, t_blk, V_PAD), lambda b, p, i: (b, p, k_blk0, 0)),
    ]
    args = [q, k, v]
    if diff_args is not None:
        lam_spec = pl.BlockSpec((1, 2, DIFF_QK), lambda b, p, i: (p, 0, 0))
        in_specs += [lam_spec] * 4 + [pl.BlockSpec((1, LANES), lambda b, p, i: (0, 0))]
        args += list(diff_args)
    return pl.pallas_call(
        functools.partial(_attn_kernel, n_kv=n_kv, group=group, diff=diff_args is not None, lam_init=lam_init),
        grid=(B, n_pairs, n_q),
        in_specs=in_specs,
        out_specs=pl.BlockSpec((1, tq, LANES), lambda b, p, i: (b, i, p)),
        out_shape=jax.ShapeDtypeStruct((B, n_q * tq, n_pairs * LANES), BF16),
        compiler_params=_cparams(("parallel", "parallel", "arbitrary")),
        name=name,
    )(*args)


def _na_kernel(q_ref, k_ref, v_ref, bias_ref, o_ref, *, n_lat, win):
    g = pl.program_id(1)
    rows = n_lat // GRID_W
    start = jnp.clip(NA_GROUP_ROWS * g - NA_ROWS // 2, 0, rows - NA_WIN_ROWS) * GRID_W
    start = pl.multiple_of(start, GRID_W)
    tq = q_ref.shape[2]
    lane = lax.broadcasted_iota(jnp.int32, (tq, LANES), 1)
    outs = []
    for h in range(NA_HEADS):
        q = q_ref[0, h]
        s_l = lax.dot_general(q, k_ref[0, h, pl.ds(start, win), :], NT_DIMS, preferred_element_type=F32)
        s_l = s_l + bias_ref[0, h]
        s_c = lax.dot_general(q, k_ref[0, h, n_lat:, :], NT_DIMS, preferred_element_type=F32)
        m = jnp.maximum(jnp.max(s_l, axis=-1, keepdims=True), jnp.max(s_c, axis=-1, keepdims=True))
        p_l = jnp.exp(s_l - m).astype(BF16)
        p_c = jnp.exp(s_c - m).astype(BF16)
        oe = _bdot(p_l, v_ref[0, h, pl.ds(start, win), :]) + _bdot(p_c, v_ref[0, h, n_lat:, :])
        outs.append(oe[:, :LANES] / oe[:, LANES:])
    out = jnp.concatenate([jnp.where(lane < 64, outs[0], outs[1]), jnp.where(lane < 64, outs[2], outs[3])], axis=-1)
    o_ref[0] = out.astype(BF16)


def _na_call(q, k, v, bias, n_lat):
    B, H, T, _ = q.shape
    tq = NA_GROUP_ROWS * GRID_W
    win = NA_WIN_ROWS * GRID_W
    n_groups = n_lat // tq

    def variant(g):
        return jnp.where(g == 0, 0, jnp.where(g == n_groups - 1, 2, 1))

    return pl.pallas_call(
        functools.partial(_na_kernel, n_lat=n_lat, win=win),
        grid=(B, n_groups),
        in_specs=[
            pl.BlockSpec((1, H, tq, HEAD_PAD), lambda b, g: (b, 0, g, 0)),
            pl.BlockSpec((1, H, T, HEAD_PAD), lambda b, g: (b, 0, 0, 0)),
            pl.BlockSpec((1, H, T, V_PAD), lambda b, g: (b, 0, 0, 0)),
            pl.BlockSpec((1, H, tq, win), lambda b, g: (variant(g), 0, 0, 0)),
        ],
        out_specs=pl.BlockSpec((1, tq, H * 64), lambda b, g: (b, g, 0)),
        out_shape=jax.ShapeDtypeStruct((B, n_lat, H * 64), BF16),
        compiler_params=_cparams(("parallel", "arbitrary")),
        name="na_attn",
    )(q, k, v, bias)


def _merge_kernel(x_ref, mod_ref, ya_ref, yb_ref, yc_ref, yd_ref, wg_ref, wb_ref, wo_ref, lg_ref, lb_ref, o_ref):
    D = D_MODEL
    x = x_ref[0]
    mod = mod_ref[0]
    sh, sc, gate_res = mod[:, 0:D], mod[:, D:2 * D], mod[:, 2 * D:3 * D]
    h = (x * (1.0 + sc) + sh).astype(BF16)
    acc = None
    for i, y_ref in enumerate((ya_ref, yb_ref, yc_ref, yd_ref)):
        gate = _sigmoid(_bdot(h, wg_ref[:, i * D:(i + 1) * D]))
        term = gate * _bdot(y_ref[0], wb_ref[i])
        acc = term if acc is None else acc + term
    out = _bdot(acc.astype(BF16), wo_ref[...])
    o_ref[0] = _layer_norm(DEEPNORM_ALPHA * x + gate_res * out, lg_ref[...], lb_ref[...])


def _merge_call(x, mod, ys, wg, wb, wo, lg, lb, tm):
    B, S, D = x.shape
    per_batch_mod = mod.shape[0] > 1
    tok = lambda w: pl.BlockSpec((1, tm, w), lambda b, i: (b, i, 0))
    return pl.pallas_call(
        _merge_kernel,
        grid=(B, S // tm),
        in_specs=[tok(D), pl.BlockSpec((1, 1, 6 * D), lambda b, i: (b if per_batch_mod else 0, 0, 0))]
        + [tok(BRANCH_W)] * 4
        + [_const_spec(wg.shape), _const_spec(wb.shape), _const_spec(wo.shape), _const_spec(lg.shape), _const_spec(lb.shape)],
        out_specs=tok(D),
        out_shape=jax.ShapeDtypeStruct((B, S, D), F32),
        compiler_params=_cparams(("parallel", "arbitrary")),
        name="merge_ln",
    )(x, mod, *ys, wg, wb, wo, lg, lb)


def _ffn_kernel(x_ref, mod_ref, wgu_ref, wd_ref, lg_ref, lb_ref, o_ref):
    D = D_MODEL
    x = x_ref[0]
    mod = mod_ref[0]
    sh, sc, gate_res = mod[:, 3 * D:4 * D], mod[:, 4 * D:5 * D], mod[:, 5 * D:6 * D]
    h = (x * (1.0 + sc) + sh).astype(BF16)
    gu = _bdot(h, wgu_ref[...])
    g, u = gu[:, :D_FF], gu[:, D_FF:]
    act = (g * _sigmoid(g) * u).astype(BF16)
    down = _bdot(act, wd_ref[...])
    o_ref[0] = _layer_norm(DEEPNORM_ALPHA * x + gate_res * down, lg_ref[...], lb_ref[...])


def _ffn_call(x, mod, wgu, wd, lg, lb, tm):
    B, S, D = x.shape
    per_batch_mod = mod.shape[0] > 1
    tok = pl.BlockSpec((1, tm, D), lambda b, i: (b, i, 0))
    return pl.pallas_call(
        _ffn_kernel,
        grid=(B, S // tm),
        in_specs=[tok, pl.BlockSpec((1, 1, 6 * D), lambda b, i: (b if per_batch_mod else 0, 0, 0)),
                  _const_spec(wgu.shape), _const_spec(wd.shape), _const_spec(lg.shape), _const_spec(lb.shape)],
        out_specs=tok,
        out_shape=jax.ShapeDtypeStruct((B, S, D), F32),
        compiler_params=_cparams(("parallel", "arbitrary")),
        name="ffn_ln",
    )(x, mod, wgu, wd, lg, lb)


def kernel(x, c, ctx, c_ctx, w_ada, b_ada, w_in, g_q_a, w_q_up, g_kv_a, w_kv_up, rpb, lam_q1, lam_k1, lam_q2, lam_k2,
           g_sub, g_qn, g_kn, w_branch, w_out, ln1_g, ln1_b, w_gate_up, w_down, ln2_g, ln2_b):
    B, S, D = x.shape
    rows = S // GRID_W
    n_q = S // Q_TILE
    T = S + CTX_LEN
    ctx_blk = S // CTX_LEN

    c8 = jnp.concatenate([c, c_ctx[None, :], jnp.zeros((8 - B - 1, D), F32)], axis=0)
    mod = _ada_call(c8, w_ada, b_ada)
    tabs = _rope_tables(S)
    small_cols, qup_cols = _small_cols(), _qup_cols()
    kvk_cols, kvv_cols = _kvup_cols()
    row = lambda v: v.reshape(1, -1)
    pad64 = lambda v: jnp.concatenate([v, jnp.zeros_like(v)]).reshape(1, LANES)

    xl, xc = x, ctx
    for l in range(DEPTH):
        need_ctx = l < DEPTH - 1
        lam_init = 0.8 - 0.6 * math.exp(-0.3 * l)
        mod_l = mod[l, :B].reshape(B, 1, 6 * D)
        mod_c = mod[l, B:B + 1].reshape(1, 1, 6 * D)
        wsm = _gather_cols(w_in[l], small_cols).astype(BF16)
        wg = w_in[l][:, GATE_OFF:].astype(BF16)
        wqup = _gather_cols(w_q_up[l], qup_cols).astype(BF16)
        wkvk = _gather_cols(w_kv_up[l], kvk_cols).astype(BF16)
        wkvv = _gather_cols(w_kv_up[l], kvv_cols).astype(BF16)

        (qa, ka, va, qn, kn, vn, qd, kd, vd, qg, kg, vg) = _proj_call(
            xl, xc, mod_l, mod_c, wsm, row(g_q_a[l]), wqup, row(g_kv_a[l]), wkvk, wkvv,
            pad64(g_qn[l]), pad64(g_kn[l]), tabs)

        diff_args = (lam_q1[l].reshape(2, 2, DIFF_QK), lam_k1[l].reshape(2, 2, DIFF_QK),
                     lam_q2[l].reshape(2, 2, DIFF_QK), lam_k2[l].reshape(2, 2, DIFF_QK),
                     jnp.concatenate([g_sub[l], g_sub[l]]).reshape(1, LANES))
        bias = _na_bias_tables(rpb[l], rows)

        lat = dict(tq=Q_TILE, q_blk0=0, n_q=n_q, t_blk=T, k_blk0=0)
        ya = _attn_call(qa, ka, va, group=1, n_kv=2, name="mla_attn", **lat)
        yb = _na_call(qn, kn, vn, bias, S)
        yc = _attn_call(qd, kd, vd, group=2, n_kv=2, diff_args=diff_args, lam_init=lam_init, name="diff_attn", **lat)
        yd = _attn_call(qg, kg, vg, group=2, n_kv=1, name="gqa_attn", **lat)

        wb = w_branch[l].astype(BF16)
        wo = w_out[l].astype(BF16)
        wgu = w_gate_up[l].astype(BF16)
        wd = w_down[l].astype(BF16)
        ln1 = (row(ln1_g[l]), row(ln1_b[l]))
        ln2 = (row(ln2_g[l]), row(ln2_b[l]))

        if need_ctx:
            cx = dict(tq=CTX_LEN, q_blk0=ctx_blk, n_q=1, t_blk=CTX_LEN, k_blk0=ctx_blk)
            ya_c = _attn_call(qa, ka, va, group=1, n_kv=2, name="mla_attn_ctx", **cx)
            yb_c = _attn_call(qn, kn, vn, group=1, n_kv=2, name="na_attn_ctx", **cx)
            yc_c = _attn_call(qd, kd, vd, group=2, n_kv=2, diff_args=diff_args, lam_init=lam_init,
                              name="diff_attn_ctx", **cx)
            yd_c = _attn_call(qg, kg, vg, group=2, n_kv=1, name="gqa_attn_ctx", **cx)
            xc1 = _merge_call(xc, mod_c, (ya_c, yb_c, yc_c, yd_c), wg, wb, wo, *ln1, tm=CTX_LEN)
            xc = _ffn_call(xc1, mod_c, wgu, wd, *ln2, tm=CTX_LEN)

        xl1 = _merge_call(xl, mod_l, (ya, yb, yc, yd), wg, wb, wo, *ln1, tm=Q_TILE)
        xl = _ffn_call(xl1, mod_l, wgu, wd, *ln2, tm=Q_TILE)
    return xl
```

```python
import functools
import math

import jax
import jax.numpy as jnp
import numpy as np
from jax import lax
from jax.experimental import pallas as pl
from jax.experimental.pallas import tpu as pltpu

F32 = jnp.float32
BF16 = jnp.bfloat16

D_MODEL = 1024
DEPTH = 2
CTX_LEN = 256
GRID_W = 64
ROPE_THETA = 10000.0
EPS = 1e-6

MLA_HEADS = 4
MLA_Q_RANK = 384
MLA_KV_RANK = 256
MLA_NOPE = 64
MLA_ROPE = 32
MLA_V = 64
NA_HEADS = 4
NA_DIM = 64
NA_ROWS = 8
NA_COLS = 16
DIFF_HEADS = 4
DIFF_QK = 32
DIFF_V = 64
GQA_HEADS = 4
GQA_KV_HEADS = 2
GQA_DIM = 64
N_BRANCH = 4
BRANCH_W = 256
D_FF = -(-8 * D_MODEL // (3 * 256)) * 256
DEEPNORM_ALPHA = (2 * DEPTH) ** 0.25

IN_SIZES = (
    MLA_Q_RANK, MLA_KV_RANK, MLA_ROPE,
    NA_HEADS * NA_DIM, NA_HEADS * NA_DIM, NA_HEADS * NA_DIM,
    DIFF_HEADS * 2 * DIFF_QK, DIFF_HEADS * 2 * DIFF_QK, DIFF_HEADS * DIFF_V,
    GQA_HEADS * GQA_DIM, GQA_KV_HEADS * GQA_DIM, GQA_KV_HEADS * GQA_DIM,
    N_BRANCH * D_MODEL,
)
IN_OFFS = tuple(int(v) for v in np.concatenate([[0], np.cumsum(IN_SIZES)]))
GATE_OFF = IN_OFFS[12]

LANES = 128
HEAD_PAD = LANES
V_PAD = 2 * LANES
TOK_TILE = 256
Q_TILE = 512
NA_GROUP_ROWS = 4
NA_WIN_ROWS = 12
ROW_SPLIT = 2
MASK_VALUE = -1e30
VMEM_LIMIT = 56 * 1024 * 1024

MLA_SCALE = (MLA_NOPE + MLA_ROPE) ** -0.5
NA_SCALE = NA_DIM ** -0.5
DIFF_SCALE = DIFF_QK ** -0.5
GQA_SCALE = GQA_DIM ** -0.5

P_CQ = 0
P_CKV = P_CQ + MLA_Q_RANK
P_KPE = P_CKV + MLA_KV_RANK
P_QN = P_KPE + HEAD_PAD
P_KN = P_QN + NA_HEADS * HEAD_PAD
P_VN = P_KN + NA_HEADS * HEAD_PAD
P_QD = P_VN + NA_HEADS * HEAD_PAD
P_KD = P_QD + DIFF_HEADS * HEAD_PAD
P_VD = P_KD + DIFF_HEADS * HEAD_PAD
P_QG = P_VD + DIFF_HEADS * HEAD_PAD
P_KG = P_QG + GQA_HEADS * HEAD_PAD
P_VG = P_KG + GQA_KV_HEADS * HEAD_PAD
P_END = P_VG + GQA_KV_HEADS * HEAD_PAD

NT_DIMS = (((1,), (1,)), ((), ()))


def _cparams(sem):
    return pltpu.CompilerParams(dimension_semantics=sem, vmem_limit_bytes=VMEM_LIMIT)


def _const_spec(shape):
    nd = len(shape)
    return pl.BlockSpec(shape, lambda *_: (0,) * nd, pipeline_mode=pl.Buffered(1))


def _head_cols(base, n_heads, dup):
    cols = []
    for h in range(n_heads):
        c = list(range(base + 64 * h, base + 64 * h + 64))
        cols += c + (c if dup else [-1] * 64)
    return cols


def _small_cols():
    o = IN_OFFS
    cols = list(range(o[0], o[1])) + list(range(o[1], o[2]))
    cols += [-1] * 64 + list(range(o[2], o[3])) + [-1] * 32
    cols += _head_cols(o[3], NA_HEADS, False) + _head_cols(o[4], NA_HEADS, False) + _head_cols(o[5], NA_HEADS, True)
    cols += _head_cols(o[6], DIFF_HEADS, False) + _head_cols(o[7], DIFF_HEADS, False) + _head_cols(o[8], DIFF_HEADS, True)
    cols += _head_cols(o[9], GQA_HEADS, False) + _head_cols(o[10], GQA_KV_HEADS, False) + _head_cols(o[11], GQA_KV_HEADS, True)
    assert len(cols) == P_END
    return np.asarray(cols, np.int32)


def _gather_cols(w, idx):
    valid = jnp.asarray(idx >= 0)
    return jnp.where(valid[None, :], jnp.take(w, jnp.asarray(np.maximum(idx, 0)), axis=1), 0.0)


def _qup_cols():
    cols = []
    for h in range(MLA_HEADS):
        cols += list(range(96 * h, 96 * h + 96)) + [-1] * 32
    return np.asarray(cols, np.int32)


def _kvup_cols():
    kc, vc = [], []
    for h in range(MLA_HEADS):
        kc += list(range(128 * h, 128 * h + 64)) + [-1] * 64
        v = list(range(128 * h + 64, 128 * h + 128))
        vc += v + v
    return np.asarray(kc, np.int32), np.asarray(vc, np.int32)


def _rope_pattern(S, rot_dim):
    t = jnp.arange(S)
    pos = jnp.stack([t // GRID_W, t % GRID_W], axis=-1).astype(F32)
    n_f = rot_dim // 4
    inv = ROPE_THETA ** (-jnp.arange(n_f, dtype=F32) / n_f)
    ang = pos[:, :, None] * inv
    cos, sin = jnp.cos(ang), jnp.sin(ang)
    c = jnp.concatenate([cos[:, 0], cos[:, 0], cos[:, 1], cos[:, 1]], axis=-1)
    s = jnp.concatenate([-sin[:, 0], sin[:, 0], -sin[:, 1], sin[:, 1]], axis=-1)
    c = jnp.concatenate([c, jnp.ones((CTX_LEN, rot_dim), F32)], axis=0)
    s = jnp.concatenate([s, jnp.zeros((CTX_LEN, rot_dim), F32)], axis=0)
    return c, s


def _rope_tables(S):
    T = S + CTX_LEN
    c32, s32 = _rope_pattern(S, 32)
    c64, s64 = _rope_pattern(S, 64)
    one = lambda n: jnp.ones((T, n), F32)
    zero = lambda n: jnp.zeros((T, n), F32)
    cat = lambda xs: jnp.concatenate(xs, axis=-1)
    ca, sa = cat([one(64), c32, one(32)]), cat([zero(64), s32, zero(32)])
    cd, sd = cat([c32, c32, one(64)]), cat([s32, s32, zero(64)])
    cg, sg = cat([c64, one(64)]), cat([s64, zero(64)])
    return ca, sa, cd, sd, cg, sg


def _na_bias_tables(rpb_l, rows):
    n_heads = rpb_l.shape[0]
    n_groups = rows // NA_GROUP_ROWS
    kh, kw = min(NA_ROWS, rows), NA_COLS
    w = np.arange(GRID_W)[:, None]
    c = np.arange(GRID_W)[None, :]
    cs = np.clip(w - kw // 2, 0, GRID_W - kw)
    col_valid = (c >= cs) & (c < cs + kw)
    col_off = c - w + (NA_COLS - 1)
    onehot_c = ((col_off[None] == np.arange(2 * NA_COLS - 1)[:, None, None]) & col_valid[None]).astype(np.float32)
    a = np.arange(NA_GROUP_ROWS)[:, None]
    i = np.arange(NA_WIN_ROWS)[None, :]
    out = []
    for g in (0, 1, n_groups - 1):
        r = NA_GROUP_ROWS * g + a
        i_abs = np.clip(NA_GROUP_ROWS * g - kh // 2, 0, rows - NA_WIN_ROWS) + i
        rs = np.clip(r - kh // 2, 0, rows - kh)
        row_valid = (i_abs >= rs) & (i_abs < rs + kh)
        row_off = i_abs - r + (NA_ROWS - 1)
        onehot_r = ((row_off[..., None] == np.arange(2 * NA_ROWS - 1)) & row_valid[..., None]).astype(np.float32)
        valid = row_valid[:, None, :, None] & col_valid[None, :, None, :]
        b = jnp.einsum('air,hrk,kwc->hawic', jnp.asarray(onehot_r), rpb_l.astype(F32), jnp.asarray(onehot_c),
                       precision=lax.Precision.HIGHEST)
        b = jnp.where(jnp.asarray(valid)[None], b, MASK_VALUE)
        out.append(b.reshape(n_heads, NA_GROUP_ROWS * GRID_W, NA_WIN_ROWS * GRID_W))
    return jnp.stack(out, axis=0)


def _sigmoid(x):
    return 1.0 / (1.0 + jnp.exp(-x))


def _rms(x, g, n):
    ms = jnp.sum(x * x, axis=-1, keepdims=True) * (1.0 / n)
    return x * lax.rsqrt(ms + EPS) * g


def _layer_norm(x, g, b):
    mu = jnp.mean(x, axis=-1, keepdims=True)
    xc = x - mu
    var = jnp.mean(xc * xc, axis=-1, keepdims=True)
    return xc * lax.rsqrt(var + EPS) * g + b


def _rope(z, c, s, n_f):
    lane = lax.broadcasted_iota(jnp.int32, z.shape, 1)
    first = (lane & n_f) == 0
    up = pltpu.roll(z, LANES - n_f, 1)
    dn = pltpu.roll(z, n_f, 1)
    return z * c + jnp.where(first, up, dn) * s


def _bdot(a, b):
    return jnp.dot(a, b, preferred_element_type=F32)


def _ada_kernel(c_ref, w_ref, b_ref, o_ref):
    c = c_ref[...]
    a = c * _sigmoid(c)
    o_ref[0] = jnp.dot(a, w_ref[0], preferred_element_type=F32, precision=lax.Precision.HIGHEST) + b_ref[0]


def _ada_call(c8, w_ada, b_ada):
    L, D, N = w_ada.shape
    tn = 1536
    return pl.pallas_call(
        _ada_kernel,
        grid=(L, N // tn),
        in_specs=[
            pl.BlockSpec((8, D), lambda l, n: (0, 0)),
            pl.BlockSpec((1, D, tn), lambda l, n: (l, 0, n)),
            pl.BlockSpec((1, 1, tn), lambda l, n: (l, 0, n)),
        ],
        out_specs=pl.BlockSpec((1, 8, tn), lambda l, n: (l, 0, n)),
        out_shape=jax.ShapeDtypeStruct((L, 8, N), F32),
        compiler_params=_cparams(("parallel", "parallel")),
        name="ada_mod",
    )(c8, w_ada, b_ada.reshape(L, 1, N))


def _proj_kernel(x_ref, ctx_ref, modl_ref, modc_ref, w_ref, gqa_ref, wqup_ref, gkva_ref, wkvk_ref, wkvv_ref,
                 gqn_ref, gkn_ref, ca_ref, sa_ref, cd_ref, sd_ref, cg_ref, sg_ref,
                 qa_ref, ka_ref, va_ref, qn_ref, kn_ref, vn_ref, qd_ref, kd_ref, vd_ref, qg_ref, kg_ref, vg_ref,
                 *, n_lat_tiles):
    D = D_MODEL
    is_ctx = pl.program_id(0) == n_lat_tiles
    x = jnp.where(is_ctx, ctx_ref[0], x_ref[0])
    ml, mc = modl_ref[0], modc_ref[0]
    sh = jnp.where(is_ctx, mc[:, 0:D], ml[:, 0:D])
    sc = jnp.where(is_ctx, mc[:, D:2 * D], ml[:, D:2 * D])
    h = (x * (1.0 + sc) + sh).astype(BF16)
    z = _bdot(h, w_ref[...])
    tm = z.shape[0]
    ones = jnp.ones((tm, LANES), F32)
    lane = lax.broadcasted_iota(jnp.int32, (tm, LANES), 1)
    head = lambda base, i: z[:, base + HEAD_PAD * i: base + HEAD_PAD * (i + 1)]
    with_ones = lambda v: jnp.concatenate([v, ones], axis=-1).astype(BF16)

    ca, sa = ca_ref[...], sa_ref[...]
    cqn = _rms(z[:, P_CQ:P_CKV], gqa_ref[...], MLA_Q_RANK).astype(BF16)
    qa_full = _bdot(cqn, wqup_ref[...])
    ckvn = _rms(z[:, P_CKV:P_KPE], gkva_ref[...], MLA_KV_RANK).astype(BF16)
    k_full = _bdot(ckvn, wkvk_ref[...])
    v_full = _bdot(ckvn, wkvv_ref[...])
    kpe = _rope(z[:, P_KPE:P_QN], ca, sa, MLA_ROPE // 4)
    for i in range(MLA_HEADS):
        sl = slice(HEAD_PAD * i, HEAD_PAD * (i + 1))
        qa_ref[0, i] = (_rope(qa_full[:, sl], ca, sa, MLA_ROPE // 4) * MLA_SCALE).astype(BF16)
        ka_ref[0, i] = (k_full[:, sl] + kpe).astype(BF16)
        va_ref[0, i] = with_ones(v_full[:, sl])

    for i in range(NA_HEADS):
        qn_ref[0, i] = (head(P_QN, i) * NA_SCALE).astype(BF16)
        kn_ref[0, i] = head(P_KN, i).astype(BF16)
        vn_ref[0, i] = with_ones(head(P_VN, i))

    cd, sd = cd_ref[...], sd_ref[...]
    for i in range(DIFF_HEADS):
        zq = _rope(head(P_QD, i), cd, sd, DIFF_QK // 4) * DIFF_SCALE
        qd_ref[0, 2 * i] = jnp.where(lane < DIFF_QK, zq, 0.0).astype(BF16)
        qd_ref[0, 2 * i + 1] = jnp.where(lane >= DIFF_QK, zq, 0.0).astype(BF16)
        kd_ref[0, i] = _rope(head(P_KD, i), cd, sd, DIFF_QK // 4).astype(BF16)
        vd_ref[0, i] = with_ones(head(P_VD, i))

    cg, sg = cg_ref[...], sg_ref[...]
    gqn, gkn = gqn_ref[...], gkn_ref[...]
    for i in range(GQA_HEADS):
        qn = _rms(head(P_QG, i), gqn, GQA_DIM)
        qg_ref[0, i] = (_rope(qn, cg, sg, GQA_DIM // 4) * GQA_SCALE).astype(BF16)
    for i in range(GQA_KV_HEADS):
        kn = _rms(head(P_KG, i), gkn, GQA_DIM)
        kg_ref[0, i] = _rope(kn, cg, sg, GQA_DIM // 4).astype(BF16)
        vg_ref[0, i] = with_ones(head(P_VG, i))


def _proj_call(xl, xc, mod_l, mod_c, wsm, gqa, wqup, gkva, wkvk, wkvv, gqn, gkn, tabs):
    B, S, D = xl.shape
    tm = TOK_TILE
    n_lat = S // tm
    T = S + CTX_LEN
    grid = (n_lat + 1, B)
    tab_spec = pl.BlockSpec((tm, LANES), lambda t, b: (t, 0))
    in_specs = [
        pl.BlockSpec((1, tm, D), lambda t, b: (b, jnp.minimum(t, n_lat - 1), 0)),
        pl.BlockSpec((1, tm, D), lambda t, b: (b, 0, 0)),
        pl.BlockSpec((1, 1, 6 * D), lambda t, b: (b, 0, 0)),
        pl.BlockSpec((1, 1, 6 * D), lambda t, b: (0, 0, 0)),
        _const_spec(wsm.shape), _const_spec(gqa.shape), _const_spec(wqup.shape), _const_spec(gkva.shape),
        _const_spec(wkvk.shape), _const_spec(wkvv.shape), _const_spec(gqn.shape), _const_spec(gkn.shape),
    ] + [tab_spec] * 6

    def o(n_heads, width):
        return (jax.ShapeDtypeStruct((B, n_heads, T, width), BF16),
                pl.BlockSpec((1, n_heads, tm, width), lambda t, b: (b, 0, t, 0)))

    outs = [o(MLA_HEADS, HEAD_PAD), o(MLA_HEADS, HEAD_PAD), o(MLA_HEADS, V_PAD),
            o(NA_HEADS, HEAD_PAD), o(NA_HEADS, HEAD_PAD), o(NA_HEADS, V_PAD),
            o(2 * DIFF_HEADS, HEAD_PAD), o(DIFF_HEADS, HEAD_PAD), o(DIFF_HEADS, V_PAD),
            o(GQA_HEADS, HEAD_PAD), o(GQA_KV_HEADS, HEAD_PAD), o(GQA_KV_HEADS, V_PAD)]
    return pl.pallas_call(
        functools.partial(_proj_kernel, n_lat_tiles=n_lat),
        grid=grid,
        in_specs=in_specs,
        out_specs=[s for _, s in outs],
        out_shape=[s for s, _ in outs],
        compiler_params=_cparams(("arbitrary", "arbitrary")),
        name="in_proj",
    )(xl, xc, mod_l, mod_c, wsm, gqa, wqup, gkva, wkvk, wkvv, gqn, gkn, *tabs)


def _softmax_pv(q, k, v):
    rows = q.shape[0] // ROW_SPLIT
    scores = [lax.dot_general(q[r * rows:(r + 1) * rows], k, NT_DIMS, preferred_element_type=F32)
              for r in range(ROW_SPLIT)]
    outs = []
    for s in scores:
        m = jnp.max(s, axis=-1, keepdims=True)
        p = jnp.exp(s - m).astype(BF16)
        oe = _bdot(p, v)
        outs.append(oe[:, :LANES] / oe[:, LANES:])
    return jnp.concatenate(outs, axis=0)


def _attn_kernel(*refs, n_kv, group, diff, lam_init):
    if diff:
        q_ref, k_ref, v_ref, lq1_ref, lk1_ref, lq2_ref, lk2_ref, gsub_ref, o_ref = refs
    else:
        q_ref, k_ref, v_ref, o_ref = refs
    tq = q_ref.shape[2]
    heads = []
    for j in range(n_kv):
        q = q_ref[0, j * group:(j + 1) * group].reshape(group * tq, HEAD_PAD)
        on = _softmax_pv(q, k_ref[0, j], v_ref[0, j])
        heads += [on[g * tq:(g + 1) * tq] for g in range(group)]
    if diff:
        ys = []
        for j in range(2):
            dot_l = lambda a, b: jnp.sum(a[0, j:j + 1] * b[0, j:j + 1], axis=-1, keepdims=True)
            lam = jnp.exp(dot_l(lq1_ref, lk1_ref)) - jnp.exp(dot_l(lq2_ref, lk2_ref)) + lam_init
            y = heads[2 * j] - lam * heads[2 * j + 1]
            ys.append(_rms(y, gsub_ref[...], LANES) * (1.0 - lam_init))
        heads = ys
    lane = lax.broadcasted_iota(jnp.int32, (tq, LANES), 1)
    o_ref[0] = jnp.where(lane < 64, heads[0], heads[1]).astype(BF16)


def _attn_call(q, k, v, *, group, n_kv, tq, q_blk0, n_q, t_blk, k_blk0, diff_args=None, lam_init=0.0, name):
    B, Hq = q.shape[0], q.shape[1]
    n_pairs = Hq // (n_kv * group)
    qh = n_kv * group
    in_specs = [
        pl.BlockSpec((1, qh, tq, HEAD_PAD), lambda b, p, i: (b, p, q_blk0 + i, 0)),
        pl.BlockSpec((1, n_kv, t_blk, HEAD_PAD), lambda b, p, i: (b, p, k_blk0, 0)),
        pl.BlockSpec((1, n_kv, t_blk, V_PAD), lambda b, p, i: (b, p, k_blk0, 0)),
    ]
    args = [q, k, v]
    if diff_args is not None:
        lam_spec = pl.BlockSpec((1, 2, DIFF_QK), lambda b, p, i: (p, 0, 0))
        in_specs += [lam_spec] * 4 + [pl.BlockSpec((1, LANES), lambda b, p, i: (0, 0))]
        args += list(diff_args)
    return pl.pallas_call(
        functools.partial(_attn_kernel, n_kv=n_kv, group=group, diff=diff_args is not None, lam_init=lam_init),
        grid=(B, n_pairs, n_q),
        in_specs=in_specs,
        out_specs=pl.BlockSpec((1, tq, LANES), lambda b, p, i: (b, i, p)),
        out_shape=jax.ShapeDtypeStruct((B, n_q * tq, n_pairs * LANES), BF16),
        compiler_params=_cparams(("parallel", "parallel", "arbitrary")),
        name=name,
    )(*args)


def _na_kernel(q_ref, k_ref, v_ref, bias_ref, o_ref, *, n_lat, win):
    g = pl.program_id(1)
    rows = n_lat // GRID_W
    start = jnp.clip(NA_GROUP_ROWS * g - NA_ROWS // 2, 0, rows - NA_WIN_ROWS) * GRID_W
    start = pl.multiple_of(start, GRID_W)
    tq = q_ref.shape[2]
    lane = lax.broadcasted_iota(jnp.int32, (tq, LANES), 1)
    outs = []
    for h in range(NA_HEADS):
        q = q_ref[0, h]
        s_l = lax.dot_general(q, k_ref[0, h, pl.ds(start, win), :], NT_DIMS, preferred_element_type=F32)
        s_l = s_l + bias_ref[0, h]
        s_c = lax.dot_general(q, k_ref[0, h, n_lat:, :], NT_DIMS, preferred_element_type=F32)
        m = jnp.maximum(jnp.max(s_l, axis=-1, keepdims=True), jnp.max(s_c, axis=-1, keepdims=True))
        p_l = jnp.exp(s_l - m).astype(BF16)
        p_c = jnp.exp(s_c - m).astype(BF16)
        oe = _bdot(p_l, v_ref[0, h, pl.ds(start, win), :]) + _bdot(p_c, v_ref[0, h, n_lat:, :])
        outs.append(oe[:, :LANES] / oe[:, LANES:])
    out = jnp.concatenate([jnp.where(lane < 64, outs[0], outs[1]), jnp.where(lane < 64, outs[2], outs[3])], axis=-1)
    o_ref[0] = out.astype(BF16)


def _na_call(q, k, v, bias, n_lat):
    B, H, T, _ = q.shape
    tq = NA_GROUP_ROWS * GRID_W
    win = NA_WIN_ROWS * GRID_W
    n_groups = n_lat // tq

    def variant(g):
        return jnp.where(g == 0, 0, jnp.where(g == n_groups - 1, 2, 1))

    return pl.pallas_call(
        functools.partial(_na_kernel, n_lat=n_lat, win=win),
        grid=(B, n_groups),
        in_specs=[
            pl.BlockSpec((1, H, tq, HEAD_PAD), lambda b, g: (b, 0, g, 0)),
            pl.BlockSpec((1, H, T, HEAD_PAD), lambda b, g: (b, 0, 0, 0)),
            pl.BlockSpec((1, H, T, V_PAD), lambda b, g: (b, 0, 0, 0)),
            pl.BlockSpec((1, H, tq, win), lambda b, g: (variant(g), 0, 0, 0)),
        ],
        out_specs=pl.BlockSpec((1, tq, H * 64), lambda b, g: (b, g, 0)),
        out_shape=jax.ShapeDtypeStruct((B, n_lat, H * 64), BF16),
        compiler_params=_cparams(("parallel", "arbitrary")),
        name="na_attn",
    )(q, k, v, bias)


def _merge_kernel(x_ref, mod_ref, ya_ref, yb_ref, yc_ref, yd_ref, wg_ref, wb_ref, wo_ref, lg_ref, lb_ref, o_ref):
    D = D_MODEL
    x = x_ref[0]
    mod = mod_ref[0]
    sh, sc, gate_res = mod[:, 0:D], mod[:, D:2 * D], mod[:, 2 * D:3 * D]
    h = (x * (1.0 + sc) + sh).astype(BF16)
    acc = None
    for i, y_ref in enumerate((ya_ref, yb_ref, yc_ref, yd_ref)):
        gate = _sigmoid(_bdot(h, wg_ref[:, i * D:(i + 1) * D]))
        term = gate * _bdot(y_ref[0], wb_ref[i])
        acc = term if acc is None else acc + term
    out = _bdot(acc.astype(BF16), wo_ref[...])
    o_ref[0] = _layer_norm(DEEPNORM_ALPHA * x + gate_res * out, lg_ref[...], lb_ref[...])


def _merge_call(x, mod, ys, wg, wb, wo, lg, lb, tm):
    B, S, D = x.shape
    per_batch_mod = mod.shape[0] > 1
    tok = lambda w: pl.BlockSpec((1, tm, w), lambda b, i: (b, i, 0))
    return pl.pallas_call(
        _merge_kernel,
        grid=(B, S // tm),
        in_specs=[tok(D), pl.BlockSpec((1, 1, 6 * D), lambda b, i: (b if per_batch_mod else 0, 0, 0))]
        + [tok(BRANCH_W)] * 4
        + [_const_spec(wg.shape), _const_spec(wb.shape), _const_spec(wo.shape), _const_spec(lg.shape), _const_spec(lb.shape)],
        out_specs=tok(D),
        out_shape=jax.ShapeDtypeStruct((B, S, D), F32),
        compiler_params=_cparams(("parallel", "arbitrary")),
        name="merge_ln",
    )(x, mod, *ys, wg, wb, wo, lg, lb)


def _ffn_kernel(x_ref, mod_ref, wgu_ref, wd_ref, lg_ref, lb_ref, o_ref):
    D = D_MODEL
    x = x_ref[0]
    mod = mod_ref[0]
    sh, sc, gate_res = mod[:, 3 * D:4 * D], mod[:, 4 * D:5 * D], mod[:, 5 * D:6 * D]
    h = (x * (1.0 + sc) + sh).astype(BF16)
    gu = _bdot(h, wgu_ref[...])
    g, u = gu[:, :D_FF], gu[:, D_FF:]
    act = (g * _sigmoid(g) * u).astype(BF16)
    down = _bdot(act, wd_ref[...])
    o_ref[0] = _layer_norm(DEEPNORM_ALPHA * x + gate_res * down, lg_ref[...], lb_ref[...])


def _ffn_call(x, mod, wgu, wd, lg, lb, tm):
    B, S, D = x.shape
    per_batch_mod = mod.shape[0] > 1
    tok = pl.BlockSpec((1, tm, D), lambda b, i: (b, i, 0))
    return pl.pallas_call(
        _ffn_kernel,
        grid=(B, S // tm),
        in_specs=[tok, pl.BlockSpec((1, 1, 6 * D), lambda b, i: (b if per_batch_mod else 0, 0, 0)),
                  _const_spec(wgu.shape), _const_spec(wd.shape), _const_spec(lg.shape), _const_spec(lb.shape)],
        out_specs=tok,
        out_shape=jax.ShapeDtypeStruct((B, S, D), F32),
        compiler_params=_cparams(("parallel", "arbitrary")),
        name="ffn_ln",
    )(x, mod, wgu, wd, lg, lb)


def kernel(x, c, ctx, c_ctx, w_ada, b_ada, w_in, g_q_a, w_q_up, g_kv_a, w_kv_up, rpb, lam_q1, lam_k1, lam_q2, lam_k2,
           g_sub, g_qn, g_kn, w_branch, w_out, ln1_g, ln1_b, w_gate_up, w_down, ln2_g, ln2_b):
    B, S, D = x.shape
    rows = S // GRID_W
    n_q = S // Q_TILE
    T = S + CTX_LEN
    ctx_blk = S // CTX_LEN

    c8 = jnp.concatenate([c, c_ctx[None, :], jnp.zeros((8 - B - 1, D), F32)], axis=0)
    mod = _ada_call(c8, w_ada, b_ada)
    tabs = _rope_tables(S)
    small_cols, qup_cols = _small_cols(), _qup_cols()
    kvk_cols, kvv_cols = _kvup_cols()
    row = lambda v: v.reshape(1, -1)
    pad64 = lambda v: jnp.concatenate([v, jnp.zeros_like(v)]).reshape(1, LANES)

    xl, xc = x, ctx
    for l in range(DEPTH):
        need_ctx = l < DEPTH - 1
        lam_init = 0.8 - 0.6 * math.exp(-0.3 * l)
        mod_l = mod[l, :B].reshape(B, 1, 6 * D)
        mod_c = mod[l, B:B + 1].reshape(1, 1, 6 * D)
        wsm = _gather_cols(w_in[l], small_cols).astype(BF16)
        wg = w_in[l][:, GATE_OFF:].astype(BF16)
        wqup = _gather_cols(w_q_up[l], qup_cols).astype(BF16)
        wkvk = _gather_cols(w_kv_up[l], kvk_cols).astype(BF16)
        wkvv = _gather_cols(w_kv_up[l], kvv_cols).astype(BF16)

        (qa, ka, va, qn, kn, vn, qd, kd, vd, qg, kg, vg) = _proj_call(
            xl, xc, mod_l, mod_c, wsm, row(g_q_a[l]), wqup, row(g_kv_a[l]), wkvk, wkvv,
            pad64(g_qn[l]), pad64(g_kn[l]), tabs)

        diff_args = (lam_q1[l].reshape(2, 2, DIFF_QK), lam_k1[l].reshape(2, 2, DIFF_QK),
                     lam_q2[l].reshape(2, 2, DIFF_QK), lam_k2[l].reshape(2, 2, DIFF_QK),
                     jnp.concatenate([g_sub[l], g_sub[l]]).reshape(1, LANES))
        bias = _na_bias_tables(rpb[l], rows)

        lat = dict(tq=Q_TILE, q_blk0=0, n_q=n_q, t_blk=T, k_blk0=0)
        ya = _attn_call(qa, ka, va, group=1, n_kv=2, name="mla_attn", **lat)
        yb = _na_call(qn, kn, vn, bias, S)
        yc = _attn_call(qd, kd, vd, group=2, n_kv=2, diff_args=diff_args, lam_init=lam_init, name="diff_attn", **lat)
        yd = _attn_call(qg, kg, vg, group=2, n_kv=1, name="gqa_attn", **lat)

        wb = w_branch[l].astype(BF16)
        wo = w_out[l].astype(BF16)
        wgu = w_gate_up[l].astype(BF16)
        wd = w_down[l].astype(BF16)
        ln1 = (row(ln1_g[l]), row(ln1_b[l]))
        ln2 = (row(ln2_g[l]), row(ln2_b[l]))

        if need_ctx:
            cx = dict(tq=CTX_LEN, q_blk0=ctx_blk, n_q=1, t_blk=CTX_LEN, k_blk0=ctx_blk)
            ya_c = _attn_call(qa, ka, va, group=1, n_kv=2, name="mla_attn_ctx", **cx)
            yb_c = _attn_call(qn, kn, vn, group=1, n_kv=2, name="na_attn_ctx", **cx)
            yc_c = _attn_call(qd, kd, vd, group=2, n_kv=2, diff_args=diff_args, lam_init=lam_init,
                              name="diff_attn_ctx", **cx)
            yd_c = _attn_call(qg, kg, vg, group=2, n_kv=1, name="gqa_attn_ctx", **cx)
            xc1 = _merge_call(xc, mod_c, (ya_c, yb_c, yc_c, yd_c), wg, wb, wo, *ln1, tm=CTX_LEN)
            xc = _ffn_call(xc1, mod_c, wgu, wd, *ln2, tm=CTX_LEN)

        xl1 = _merge_call(xl, mod_l, (ya, yb, yc, yd), wg, wb, wo, *ln1, tm=Q_TILE)
        xl = _ffn_call(xl1, mod_l, wgu, wd, *ln2, tm=Q_TILE)
    return xl
```

```python
import functools
import math

import jax
import jax.numpy as jnp
import numpy as np
from jax import lax
from jax.experimental import pallas as pl
from jax.experimental.pallas import tpu as pltpu

F32 = jnp.float32
BF16 = jnp.bfloat16

D_MODEL = 1024
DEPTH = 2
CTX_LEN = 256
GRID_W = 64
ROPE_THETA = 10000.0
EPS = 1e-6

MLA_HEADS = 4
MLA_Q_RANK = 384
MLA_KV_RANK = 256
MLA_NOPE = 64
MLA_ROPE = 32
MLA_V = 64
NA_HEADS = 4
NA_DIM = 64
NA_ROWS = 8
NA_COLS = 16
DIFF_HEADS = 4
DIFF_QK = 32
DIFF_V = 64
GQA_HEADS = 4
GQA_KV_HEADS = 2
GQA_DIM = 64
N_BRANCH = 4
BRANCH_W = 256
D_FF = -(-8 * D_MODEL // (3 * 256)) * 256
DEEPNORM_ALPHA = (2 * DEPTH) ** 0.25

IN_SIZES = (
    MLA_Q_RANK, MLA_KV_RANK, MLA_ROPE,
    NA_HEADS * NA_DIM, NA_HEADS * NA_DIM, NA_HEADS * NA_DIM,
    DIFF_HEADS * 2 * DIFF_QK, DIFF_HEADS * 2 * DIFF_QK, DIFF_HEADS * DIFF_V,
    GQA_HEADS * GQA_DIM, GQA_KV_HEADS * GQA_DIM, GQA_KV_HEADS * GQA_DIM,
    N_BRANCH * D_MODEL,
)
IN_OFFS = tuple(int(v) for v in np.concatenate([[0], np.cumsum(IN_SIZES)]))
GATE_OFF = IN_OFFS[12]

LANES = 128
HEAD_PAD = LANES
V_PAD = 2 * LANES
TOK_TILE = 256
Q_TILE = 512
ATTN_TILE = 512
NA_GROUP_ROWS = 4
NA_WIN_ROWS = 12
SCORE_COLS = 256
MAX_CHAINS = 8
KEY_TILE = 256
KEY_CHUNKS = 4
V_DIM = 64
VT_ROWS = 80
MASK_VALUE = -1e30
VMEM_LIMIT = 56 * 1024 * 1024

MLA_SCALE = (MLA_NOPE + MLA_ROPE) ** -0.5
NA_SCALE = NA_DIM ** -0.5
DIFF_SCALE = DIFF_QK ** -0.5
GQA_SCALE = GQA_DIM ** -0.5

P_CQ = 0
P_CKV = P_CQ + MLA_Q_RANK
P_KPE = P_CKV + MLA_KV_RANK
P_QN = P_KPE + HEAD_PAD
P_KN = P_QN + NA_HEADS * HEAD_PAD
P_VN = P_KN + NA_HEADS * HEAD_PAD
P_QD = P_VN + NA_HEADS * HEAD_PAD
P_KD = P_QD + DIFF_HEADS * HEAD_PAD
P_VD = P_KD + DIFF_HEADS * HEAD_PAD
P_QG = P_VD + DIFF_HEADS * HEAD_PAD
P_KG = P_QG + GQA_HEADS * HEAD_PAD
P_VG = P_KG + GQA_KV_HEADS * HEAD_PAD
P_END = P_VG + GQA_KV_HEADS * HEAD_PAD

NT_DIMS = (((1,), (1,)), ((), ()))


def _cparams(sem):
    return pltpu.CompilerParams(dimension_semantics=sem, vmem_limit_bytes=VMEM_LIMIT)


def _const_spec(shape):
    nd = len(shape)
    return pl.BlockSpec(shape, lambda *_: (0,) * nd, pipeline_mode=pl.Buffered(1))


def _head_cols(base, n_heads, dup):
    cols = []
    for h in range(n_heads):
        c = list(range(base + 64 * h, base + 64 * h + 64))
        cols += c + (c if dup else [-1] * 64)
    return cols


def _small_cols():
    o = IN_OFFS
    cols = list(range(o[0], o[1])) + list(range(o[1], o[2]))
    cols += [-1] * 64 + list(range(o[2], o[3])) + [-1] * 32
    cols += _head_cols(o[3], NA_HEADS, False) + _head_cols(o[4], NA_HEADS, False) + _head_cols(o[5], NA_HEADS, True)
    cols += _head_cols(o[6], DIFF_HEADS, False) + _head_cols(o[7], DIFF_HEADS, False) + _head_cols(o[8], DIFF_HEADS, True)
    cols += _head_cols(o[9], GQA_HEADS, False) + _head_cols(o[10], GQA_KV_HEADS, False) + _head_cols(o[11], GQA_KV_HEADS, True)
    assert len(cols) == P_END
    return np.asarray(cols, np.int32)


def _gather_cols(w, idx):
    parts, i, n = [], 0, len(idx)
    while i < n:
        j = i + 1
        if idx[i] < 0:
            while j < n and idx[j] < 0:
                j += 1
            parts.append(jnp.zeros((w.shape[0], j - i), w.dtype))
        else:
            while j < n and idx[j] == idx[j - 1] + 1:
                j += 1
            parts.append(w[:, int(idx[i]):int(idx[j - 1]) + 1])
        i = j
    return jnp.concatenate(parts, axis=1)


def _qup_cols():
    cols = []
    for h in range(MLA_HEADS):
        cols += list(range(96 * h, 96 * h + 96)) + [-1] * 32
    return np.asarray(cols, np.int32)


def _kvup_cols():
    kc, vc = [], []
    for h in range(MLA_HEADS):
        kc += list(range(128 * h, 128 * h + 64)) + [-1] * 64
        v = list(range(128 * h + 64, 128 * h + 128))
        vc += v + v
    return np.asarray(kc, np.int32), np.asarray(vc, np.int32)


def _rope_pattern(S, rot_dim):
    t = jnp.arange(S)
    pos = jnp.stack([t // GRID_W, t % GRID_W], axis=-1).astype(F32)
    n_f = rot_dim // 4
    inv = ROPE_THETA ** (-jnp.arange(n_f, dtype=F32) / n_f)
    ang = pos[:, :, None] * inv
    cos, sin = jnp.cos(ang), jnp.sin(ang)
    c = jnp.concatenate([cos[:, 0], cos[:, 0], cos[:, 1], cos[:, 1]], axis=-1)
    s = jnp.concatenate([-sin[:, 0], sin[:, 0], -sin[:, 1], sin[:, 1]], axis=-1)
    c = jnp.concatenate([c, jnp.ones((CTX_LEN, rot_dim), F32)], axis=0)
    s = jnp.concatenate([s, jnp.zeros((CTX_LEN, rot_dim), F32)], axis=0)
    return c, s


def _rope_tables(S):
    T = S + CTX_LEN
    c32, s32 = _rope_pattern(S, 32)
    c64, s64 = _rope_pattern(S, 64)
    one = lambda n: jnp.ones((T, n), F32)
    zero = lambda n: jnp.zeros((T, n), F32)
    cat = lambda xs: jnp.concatenate(xs, axis=-1)
    ca, sa = cat([one(64), c32, one(32)]), cat([zero(64), s32, zero(32)])
    cd, sd = cat([c32, c32, one(64)]), cat([s32, s32, zero(64)])
    cg, sg = cat([c64, one(64)]), cat([s64, zero(64)])
    return ca, sa, cd, sd, cg, sg


def _na_bias_tables(rpb_l, rows):
    n_heads = rpb_l.shape[0]
    n_groups = rows // NA_GROUP_ROWS
    kh, kw = min(NA_ROWS, rows), NA_COLS
    w = np.arange(GRID_W)[:, None]
    c = np.arange(GRID_W)[None, :]
    cs = np.clip(w - kw // 2, 0, GRID_W - kw)
    col_valid = (c >= cs) & (c < cs + kw)
    col_off = c - w + (NA_COLS - 1)
    onehot_c = ((col_off[None] == np.arange(2 * NA_COLS - 1)[:, None, None]) & col_valid[None]).astype(np.float32)
    a = np.arange(NA_GROUP_ROWS)[:, None]
    i = np.arange(NA_WIN_ROWS)[None, :]
    out = []
    for g in (0, 1, n_groups - 1):
        r = NA_GROUP_ROWS * g + a
        i_abs = np.clip(NA_GROUP_ROWS * g - kh // 2, 0, rows - NA_WIN_ROWS) + i
        rs = np.clip(r - kh // 2, 0, rows - kh)
        row_valid = (i_abs >= rs) & (i_abs < rs + kh)
        row_off = i_abs - r + (NA_ROWS - 1)
        onehot_r = ((row_off[..., None] == np.arange(2 * NA_ROWS - 1)) & row_valid[..., None]).astype(np.float32)
        valid = row_valid[:, None, :, None] & col_valid[None, :, None, :]
        b = jnp.einsum('air,hrk,kwc->hawic', jnp.asarray(onehot_r), rpb_l.astype(F32), jnp.asarray(onehot_c),
                       precision=lax.Precision.HIGHEST)
        b = jnp.where(jnp.asarray(valid)[None], b, MASK_VALUE)
        out.append(b.reshape(n_heads, NA_GROUP_ROWS * GRID_W, NA_WIN_ROWS * GRID_W))
    return jnp.stack(out, axis=0)


def _sigmoid(x):
    return 1.0 / (1.0 + jnp.exp(-x))


def _rms(x, g, n):
    ms = jnp.sum(x * x, axis=-1, keepdims=True) * (1.0 / n)
    return x * lax.rsqrt(ms + EPS) * g


def _layer_norm(x, g, b):
    mu = jnp.mean(x, axis=-1, keepdims=True)
    xc = x - mu
    var = jnp.mean(xc * xc, axis=-1, keepdims=True)
    return xc * lax.rsqrt(var + EPS) * g + b


def _rope(z, c, s, n_f):
    lane = lax.broadcasted_iota(jnp.int32, z.shape, 1)
    first = (lane & n_f) == 0
    up = pltpu.roll(z, LANES - n_f, 1)
    dn = pltpu.roll(z, n_f, 1)
    return z * c + jnp.where(first, up, dn) * s


def _bdot(a, b):
    return jnp.dot(a, b, preferred_element_type=F32)


def _ada_kernel(c_ref, w_ref, b_ref, o_ref):
    c = c_ref[...]
    a = c * _sigmoid(c)
    o_ref[0] = jnp.dot(a, w_ref[0], preferred_element_type=F32, precision=lax.Precision.HIGHEST) + b_ref[0]


def _ada_call(c8, w_ada, b_ada):
    L, D, N = w_ada.shape
    tn = 1536
    return pl.pallas_call(
        _ada_kernel,
        grid=(L, N // tn),
        in_specs=[
            pl.BlockSpec((8, D), lambda l, n: (0, 0)),
            pl.BlockSpec((1, D, tn), lambda l, n: (l, 0, n)),
            pl.BlockSpec((1, 1, tn), lambda l, n: (l, 0, n)),
        ],
        out_specs=pl.BlockSpec((1, 8, tn), lambda l, n: (l, 0, n)),
        out_shape=jax.ShapeDtypeStruct((L, 8, N), F32),
        compiler_params=_cparams(("parallel", "parallel")),
        name="ada_mod",
    )(c8, w_ada, b_ada.reshape(L, 1, N))


def _proj_kernel(x_ref, ctx_ref, modl_ref, modc_ref, w_ref, gqa_ref, wqup_ref, gkva_ref, wkvk_ref, wkvv_ref,
                 gqn_ref, gkn_ref, ca_ref, sa_ref, cd_ref, sd_ref, cg_ref, sg_ref,
                 qa_ref, ka_ref, va_ref, qn_ref, kn_ref, vn_ref, qd_ref, kd_ref, vd_ref, qg_ref, kg_ref, vg_ref,
                 *, n_lat_tiles):
    D = D_MODEL
    is_ctx = pl.program_id(0) == n_lat_tiles
    x = jnp.where(is_ctx, ctx_ref[0], x_ref[0])
    ml, mc = modl_ref[0], modc_ref[0]
    sh = jnp.where(is_ctx, mc[:, 0:D], ml[:, 0:D])
    sc = jnp.where(is_ctx, mc[:, D:2 * D], ml[:, D:2 * D])
    h = (x * (1.0 + sc) + sh).astype(BF16)
    z = _bdot(h, w_ref[...])
    tm = z.shape[0]
    ones = jnp.ones((tm, LANES), F32)
    lane = lax.broadcasted_iota(jnp.int32, (tm, LANES), 1)
    head = lambda base, i: z[:, base + HEAD_PAD * i: base + HEAD_PAD * (i + 1)]
    with_ones = lambda v: jnp.concatenate([v, ones], axis=-1).astype(BF16)
    tail = (lax.broadcasted_iota(jnp.int32, (VT_ROWS - V_DIM, tm), 0) == 0).astype(F32)
    transposed = lambda v: jnp.concatenate([v.T[:V_DIM], tail], axis=0).astype(BF16)

    ca, sa = ca_ref[...], sa_ref[...]
    cqn = _rms(z[:, P_CQ:P_CKV], gqa_ref[...], MLA_Q_RANK).astype(BF16)
    qa_full = _bdot(cqn, wqup_ref[...])
    ckvn = _rms(z[:, P_CKV:P_KPE], gkva_ref[...], MLA_KV_RANK).astype(BF16)
    k_full = _bdot(ckvn, wkvk_ref[...])
    v_full = _bdot(ckvn, wkvv_ref[...])
    kpe = _rope(z[:, P_KPE:P_QN], ca, sa, MLA_ROPE // 4)
    for i in range(MLA_HEADS):
        sl = slice(HEAD_PAD * i, HEAD_PAD * (i + 1))
        qa_ref[0, i] = (_rope(qa_full[:, sl], ca, sa, MLA_ROPE // 4) * MLA_SCALE).astype(BF16)
        ka_ref[0, i] = (k_full[:, sl] + kpe).astype(BF16)
        va_ref[0, i] = transposed(v_full[:, sl])

    for i in range(NA_HEADS):
        qn_ref[0, i] = (head(P_QN, i) * NA_SCALE).astype(BF16)
        kn_ref[0, i] = head(P_KN, i).astype(BF16)
        vn_ref[0, i] = with_ones(head(P_VN, i))

    cd, sd = cd_ref[...], sd_ref[...]
    for i in range(DIFF_HEADS):
        zq = _rope(head(P_QD, i), cd, sd, DIFF_QK // 4) * DIFF_SCALE
        qd_ref[0, 2 * i] = jnp.where(lane < DIFF_QK, zq, 0.0).astype(BF16)
        qd_ref[0, 2 * i + 1] = jnp.where(lane >= DIFF_QK, zq, 0.0).astype(BF16)
        kd_ref[0, i] = _rope(head(P_KD, i), cd, sd, DIFF_QK // 4).astype(BF16)
        vd_ref[0, i] = transposed(head(P_VD, i))

    cg, sg = cg_ref[...], sg_ref[...]
    gqn, gkn = gqn_ref[...], gkn_ref[...]
    for i in range(GQA_HEADS):
        qn = _rms(head(P_QG, i), gqn, GQA_DIM)
        qg_ref[0, i] = (_rope(qn, cg, sg, GQA_DIM // 4) * GQA_SCALE).astype(BF16)
    for i in range(GQA_KV_HEADS):
        kn = _rms(head(P_KG, i), gkn, GQA_DIM)
        kg_ref[0, i] = _rope(kn, cg, sg, GQA_DIM // 4).astype(BF16)
        vg_ref[0, i] = transposed(head(P_VG, i))


def _proj_call(xl, xc, mod_l, mod_c, wsm, gqa, wqup, gkva, wkvk, wkvv, gqn, gkn, tabs):
    B, S, D = xl.shape
    tm = TOK_TILE
    n_lat = S // tm
    T = S + CTX_LEN
    grid = (n_lat + 1, B)
    tab_spec = pl.BlockSpec((tm, LANES), lambda t, b: (t, 0))
    in_specs = [
        pl.BlockSpec((1, tm, D), lambda t, b: (b, jnp.minimum(t, n_lat - 1), 0)),
        pl.BlockSpec((1, tm, D), lambda t, b: (b, 0, 0)),
        pl.BlockSpec((1, 1, 6 * D), lambda t, b: (b, 0, 0)),
        pl.BlockSpec((1, 1, 6 * D), lambda t, b: (0, 0, 0)),
        _const_spec(wsm.shape), _const_spec(gqa.shape), _const_spec(wqup.shape), _const_spec(gkva.shape),
        _const_spec(wkvk.shape), _const_spec(wkvv.shape), _const_spec(gqn.shape), _const_spec(gkn.shape),
    ] + [tab_spec] * 6

    def o(n_heads, width):
        return (jax.ShapeDtypeStruct((B, n_heads, T, width), BF16),
                pl.BlockSpec((1, n_heads, tm, width), lambda t, b: (b, 0, t, 0)))

    def ot(n_heads):
        return (jax.ShapeDtypeStruct((B, n_heads, VT_ROWS, T), BF16),
                pl.BlockSpec((1, n_heads, VT_ROWS, tm), lambda t, b: (b, 0, 0, t)))

    outs = [o(MLA_HEADS, HEAD_PAD), o(MLA_HEADS, HEAD_PAD), ot(MLA_HEADS),
            o(NA_HEADS, HEAD_PAD), o(NA_HEADS, HEAD_PAD), o(NA_HEADS, V_PAD),
            o(2 * DIFF_HEADS, HEAD_PAD), o(DIFF_HEADS, HEAD_PAD), ot(DIFF_HEADS),
            o(GQA_HEADS, HEAD_PAD), o(GQA_KV_HEADS, HEAD_PAD), ot(GQA_KV_HEADS)]
    return pl.pallas_call(
        functools.partial(_proj_kernel, n_lat_tiles=n_lat),
        grid=grid,
        in_specs=in_specs,
        out_specs=[s for _, s in outs],
        out_shape=[s for s, _ in outs],
        compiler_params=_cparams(("arbitrary", "arbitrary")),
        name="in_proj",
    )(xl, xc, mod_l, mod_c, wsm, gqa, wqup, gkva, wkvk, wkvv, gqn, gkn, *tabs)


def _key_chunks(t_keys):
    n_tiles = t_keys // KEY_TILE
    n_chunks = min(KEY_CHUNKS, n_tiles)
    sizes = [n_tiles // n_chunks + (1 if c < n_tiles % n_chunks else 0) for c in range(n_chunks)]
    bounds = np.cumsum([0] + sizes) * KEY_TILE
    return [(int(bounds[c]), int(bounds[c + 1])) for c in range(n_chunks)]


def _attn_kernel(*refs, n_kv, group, diff, lam_init):
    q_ref, k_ref, v_ref = refs[:3]
    sa_ref, sb_ref, ma_ref, mb_ref, p_ref = refs[-5:]
    step = pl.program_id(0)
    tq = q_ref.shape[2]
    n_dots = sa_ref.shape[0]
    cols = sa_ref.shape[2]
    split = n_dots // n_kv

    @pl.when(step == 0)
    def _():
        sb_ref[...] = jnp.zeros(sb_ref.shape, F32)
        mb_ref[...] = jnp.zeros(mb_ref.shape, F32)
        p_ref[...] = jnp.ones(p_ref.shape, BF16)

    def body(s_write, s_read, m_write, m_read):
        qs = []
        for j in range(n_kv):
            q = q_ref[0, j * group:(j + 1) * group].reshape(group * tq, HEAD_PAD)
            qs += [q[r * cols:(r + 1) * cols] for r in range(split)]
        acc = [None] * n_dots
        run_max = [None] * n_dots
        for lo, hi in _key_chunks(k_ref.shape[2]):
            for d in range(n_dots):
                j = d // split
                p_old = jnp.maximum(p_ref[d, lo:hi, :], 0.0)
                part = _bdot(v_ref[0, j, :, lo:hi], p_old)
                acc[d] = part if acc[d] is None else acc[d] + part
                p_ref[d, lo:hi, :] = jnp.exp(s_read[d, lo:hi, :] - m_read[d]).astype(BF16)
                s = lax.dot_general(k_ref[0, j, lo:hi, :], qs[d], NT_DIMS, preferred_element_type=F32)
                s_write[d, lo:hi, :] = s
                pm = jnp.max(s.reshape(MAX_CHAINS, (hi - lo) // MAX_CHAINS, cols), axis=1)
                run_max[d] = pm if run_max[d] is None else jnp.maximum(run_max[d], pm)
        heads = []
        for j in range(n_kv):
            outs = []
            for d in range(j * split, (j + 1) * split):
                m_write[d] = jnp.max(run_max[d], axis=0, keepdims=True)
                outs.append(acc[d][:V_DIM] / acc[d][V_DIM:V_DIM + 1])
            on = outs[0] if split == 1 else jnp.concatenate(outs, axis=1)
            heads += [on[:, g * tq:(g + 1) * tq] for g in range(group)]
        _attn_finish(heads, refs, diff, lam_init)

    parity = lax.rem(step, 2)

    @pl.when(parity == 0)
    def _():
        body(sa_ref, sb_ref, ma_ref, mb_ref)

    @pl.when(parity == 1)
    def _():
        body(sb_ref, sa_ref, mb_ref, ma_ref)


def _attn_finish(heads, refs, diff, lam_init):
    if diff:
        lq1_ref, lk1_ref, lq2_ref, lk2_ref, gsub_ref, o_ref = refs[3:9]
    else:
        o_ref = refs[3]
    if diff:
        ys = []
        for j in range(2):
            dot_l = lambda a, b: jnp.sum(a[0, j:j + 1] * b[0, j:j + 1], axis=-1, keepdims=True)
            lam = jnp.exp(dot_l(lq1_ref, lk1_ref)) - jnp.exp(dot_l(lq2_ref, lk2_ref)) + lam_init
            y = heads[2 * j] - lam * heads[2 * j + 1]
            ms = jnp.mean(y * y, axis=0, keepdims=True)
            ys.append(y * lax.rsqrt(ms + EPS) * gsub_ref[...] * (1.0 - lam_init))
        heads = ys
    o_ref[0] = jnp.concatenate(heads, axis=0).T.astype(BF16)


def _transpose_v(v):
    B, H, T, _ = v.shape
    vt = jnp.swapaxes(v[..., :V_DIM], -1, -2)
    tail = jnp.zeros((B, H, VT_ROWS - V_DIM, T), v.dtype).at[:, :, 0, :].set(1.0)
    return jnp.concatenate([vt, tail], axis=2)


def _attn_call(q, k, vt, *, group, n_kv, tq, q_blk0, n_q, t_blk, k_blk0, diff_args=None, lam_init=0.0, name):
    B, Hq = q.shape[0], q.shape[1]
    n_pairs = Hq // (n_kv * group)
    qh = n_kv * group
    n_units = B * n_pairs * n_q
    n_dots = qh * tq // SCORE_COLS

    def unit(u):
        return u // (n_pairs * n_q), (u // n_q) % n_pairs, u % n_q

    def stage1(s):
        return unit(jnp.minimum(s, n_units - 1))

    def stage3(s):
        return unit(jnp.maximum(s - 2, 0))

    def q_map(s):
        b, p, i = stage1(s)
        return b, p, q_blk0 + i, 0

    def k_map(s):
        b, p, _ = stage1(s)
        return b, p, k_blk0, 0

    def v_map(s):
        b, p, _ = stage3(s)
        return b, p, 0, k_blk0

    def o_map(s):
        b, p, i = stage3(s)
        return b, i, p

    in_specs = [
        pl.BlockSpec((1, qh, tq, HEAD_PAD), q_map),
        pl.BlockSpec((1, n_kv, t_blk, HEAD_PAD), k_map),
        pl.BlockSpec((1, n_kv, VT_ROWS, t_blk), v_map),
    ]
    args = [q, k, vt]
    if diff_args is not None:
        lam_spec = pl.BlockSpec((1, 2, DIFF_QK), lambda s: (stage3(s)[1], 0, 0))
        in_specs += [lam_spec] * 4 + [pl.BlockSpec((V_DIM, 1), lambda s: (0, 0))]
        args += list(diff_args)
    score_buf = pltpu.VMEM((n_dots, t_blk, SCORE_COLS), F32)
    prob_buf = pltpu.VMEM((n_dots, t_blk, SCORE_COLS), BF16)
    max_buf = pltpu.VMEM((n_dots, 1, SCORE_COLS), F32)
    return pl.pallas_call(
        functools.partial(_attn_kernel, n_kv=n_kv, group=group, diff=diff_args is not None, lam_init=lam_init),
        grid=(n_units + 2,),
        in_specs=in_specs,
        out_specs=pl.BlockSpec((1, tq, LANES), o_map),
        out_shape=jax.ShapeDtypeStruct((B, n_q * tq, n_pairs * LANES), BF16),
        scratch_shapes=[score_buf, score_buf, max_buf, max_buf, prob_buf],
        compiler_params=_cparams(("arbitrary",)),
        name=name,
    )(*args)


def _na_kernel(q_ref, k_ref, v_ref, bias_ref, o_ref, *, n_lat, win):
    g = pl.program_id(1)
    rows = n_lat // GRID_W
    start = jnp.clip(NA_GROUP_ROWS * g - NA_ROWS // 2, 0, rows - NA_WIN_ROWS) * GRID_W
    start = pl.multiple_of(start, GRID_W)
    tq = q_ref.shape[2]
    lane = lax.broadcasted_iota(jnp.int32, (tq, LANES), 1)
    outs = []
    for h in range(NA_HEADS):
        q = q_ref[0, h]
        s_l = lax.dot_general(q, k_ref[0, h, pl.ds(start, win), :], NT_DIMS, preferred_element_type=F32)
        s_l = s_l + bias_ref[0, h]
        s_c = lax.dot_general(q, k_ref[0, h, n_lat:, :], NT_DIMS, preferred_element_type=F32)
        m = jnp.maximum(jnp.max(s_l, axis=-1, keepdims=True), jnp.max(s_c, axis=-1, keepdims=True))
        p_l = jnp.exp(s_l - m).astype(BF16)
        p_c = jnp.exp(s_c - m).astype(BF16)
        oe = _bdot(p_l, v_ref[0, h, pl.ds(start, win), :]) + _bdot(p_c, v_ref[0, h, n_lat:, :])
        outs.append(oe[:, :LANES] / oe[:, LANES:])
    out = jnp.concatenate([jnp.where(lane < 64, outs[0], outs[1]), jnp.where(lane < 64, outs[2], outs[3])], axis=-1)
    o_ref[0] = out.astype(BF16)


def _na_call(q, k, v, bias, n_lat):
    B, H, T, _ = q.shape
    tq = NA_GROUP_ROWS * GRID_W
    win = NA_WIN_ROWS * GRID_W
    n_groups = n_lat // tq

    def variant(g):
        return jnp.where(g == 0, 0, jnp.where(g == n_groups - 1, 2, 1))

    return pl.pallas_call(
        functools.partial(_na_kernel, n_lat=n_lat, win=win),
        grid=(B, n_groups),
        in_specs=[
            pl.BlockSpec((1, H, tq, HEAD_PAD), lambda b, g: (b, 0, g, 0)),
            pl.BlockSpec((1, H, T, HEAD_PAD), lambda b, g: (b, 0, 0, 0)),
            pl.BlockSpec((1, H, T, V_PAD), lambda b, g: (b, 0, 0, 0)),
            pl.BlockSpec((1, H, tq, win), lambda b, g: (variant(g), 0, 0, 0)),
        ],
        out_specs=pl.BlockSpec((1, tq, H * 64), lambda b, g: (b, g, 0)),
        out_shape=jax.ShapeDtypeStruct((B, n_lat, H * 64), BF16),
        compiler_params=_cparams(("parallel", "arbitrary")),
        name="na_attn",
    )(q, k, v, bias)


def _merge_kernel(x_ref, mod_ref, ya_ref, yb_ref, yc_ref, yd_ref, wg_ref, wb_ref, wo_ref, lg_ref, lb_ref, o_ref):
    D = D_MODEL
    x = x_ref[0]
    mod = mod_ref[0]
    sh, sc, gate_res = mod[:, 0:D], mod[:, D:2 * D], mod[:, 2 * D:3 * D]
    h = (x * (1.0 + sc) + sh).astype(BF16)
    acc = None
    for i, y_ref in enumerate((ya_ref, yb_ref, yc_ref, yd_ref)):
        gate = _sigmoid(_bdot(h, wg_ref[:, i * D:(i + 1) * D]))
        term = gate * _bdot(y_ref[0], wb_ref[i])
        acc = term if acc is None else acc + term
    out = _bdot(acc.astype(BF16), wo_ref[...])
    o_ref[0] = _layer_norm(DEEPNORM_ALPHA * x + gate_res * out, lg_ref[...], lb_ref[...])


def _merge_call(x, mod, ys, wg, wb, wo, lg, lb, tm):
    B, S, D = x.shape
    per_batch_mod = mod.shape[0] > 1
    tok = lambda w: pl.BlockSpec((1, tm, w), lambda b, i: (b, i, 0))
    return pl.pallas_call(
        _merge_kernel,
        grid=(B, S // tm),
        in_specs=[tok(D), pl.BlockSpec((1, 1, 6 * D), lambda b, i: (b if per_batch_mod else 0, 0, 0))]
        + [tok(BRANCH_W)] * 4
        + [_const_spec(wg.shape), _const_spec(wb.shape), _const_spec(wo.shape), _const_spec(lg.shape), _const_spec(lb.shape)],
        out_specs=tok(D),
        out_shape=jax.ShapeDtypeStruct((B, S, D), F32),
        compiler_params=_cparams(("parallel", "arbitrary")),
        name="merge_ln",
    )(x, mod, *ys, wg, wb, wo, lg, lb)


def _ffn_kernel(x_ref, mod_ref, wgu_ref, wd_ref, lg_ref, lb_ref, o_ref):
    D = D_MODEL
    x = x_ref[0]
    mod = mod_ref[0]
    sh, sc, gate_res = mod[:, 3 * D:4 * D], mod[:, 4 * D:5 * D], mod[:, 5 * D:6 * D]
    h = (x * (1.0 + sc) + sh).astype(BF16)
    gu = _bdot(h, wgu_ref[...])
    g, u = gu[:, :D_FF], gu[:, D_FF:]
    act = (g * _sigmoid(g) * u).astype(BF16)
    down = _bdot(act, wd_ref[...])
    o_ref[0] = _layer_norm(DEEPNORM_ALPHA * x + gate_res * down, lg_ref[...], lb_ref[...])


def _ffn_call(x, mod, wgu, wd, lg, lb, tm):
    B, S, D = x.shape
    per_batch_mod = mod.shape[0] > 1
    tok = pl.BlockSpec((1, tm, D), lambda b, i: (b, i, 0))
    return pl.pallas_call(
        _ffn_kernel,
        grid=(B, S // tm),
        in_specs=[tok, pl.BlockSpec((1, 1, 6 * D), lambda b, i: (b if per_batch_mod else 0, 0, 0)),
                  _const_spec(wgu.shape), _const_spec(wd.shape), _const_spec(lg.shape), _const_spec(lb.shape)],
        out_specs=tok,
        out_shape=jax.ShapeDtypeStruct((B, S, D), F32),
        compiler_params=_cparams(("parallel", "arbitrary")),
        name="ffn_ln",
    )(x, mod, wgu, wd, lg, lb)


def kernel(x, c, ctx, c_ctx, w_ada, b_ada, w_in, g_q_a, w_q_up, g_kv_a, w_kv_up, rpb, lam_q1, lam_k1, lam_q2, lam_k2,
           g_sub, g_qn, g_kn, w_branch, w_out, ln1_g, ln1_b, w_gate_up, w_down, ln2_g, ln2_b):
    B, S, D = x.shape
    rows = S // GRID_W
    n_q = S // Q_TILE
    T = S + CTX_LEN
    ctx_blk = S // CTX_LEN

    c8 = jnp.concatenate([c, c_ctx[None, :], jnp.zeros((8 - B - 1, D), F32)], axis=0)
    mod = _ada_call(c8, w_ada, b_ada)
    tabs = _rope_tables(S)
    small_cols, qup_cols = _small_cols(), _qup_cols()
    kvk_cols, kvv_cols = _kvup_cols()
    row = lambda v: v.reshape(1, -1)
    pad64 = lambda v: jnp.concatenate([v, jnp.zeros_like(v)]).reshape(1, LANES)

    xl, xc = x, ctx
    for l in range(DEPTH):
        need_ctx = l < DEPTH - 1
        lam_init = 0.8 - 0.6 * math.exp(-0.3 * l)
        mod_l = mod[l, :B].reshape(B, 1, 6 * D)
        mod_c = mod[l, B:B + 1].reshape(1, 1, 6 * D)
        wsm = _gather_cols(w_in[l], small_cols).astype(BF16)
        wg = w_in[l][:, GATE_OFF:].astype(BF16)
        wqup = _gather_cols(w_q_up[l], qup_cols).astype(BF16)
        wkvk = _gather_cols(w_kv_up[l], kvk_cols).astype(BF16)
        wkvv = _gather_cols(w_kv_up[l], kvv_cols).astype(BF16)

        (qa, ka, vat, qn, kn, vn, qd, kd, vdt, qg, kg, vgt) = _proj_call(
            xl, xc, mod_l, mod_c, wsm, row(g_q_a[l]), wqup, row(g_kv_a[l]), wkvk, wkvv,
            pad64(g_qn[l]), pad64(g_kn[l]), tabs)

        diff_args = (lam_q1[l].reshape(2, 2, DIFF_QK), lam_k1[l].reshape(2, 2, DIFF_QK),
                     lam_q2[l].reshape(2, 2, DIFF_QK), lam_k2[l].reshape(2, 2, DIFF_QK),
                     g_sub[l].reshape(V_DIM, 1))
        bias = _na_bias_tables(rpb[l], rows)

        lat = lambda tq: dict(tq=tq, q_blk0=0, n_q=S // tq, t_blk=T, k_blk0=0)
        ya = _attn_call(qa, ka, vat, group=1, n_kv=2, name="mla_attn", **lat(ATTN_TILE))
        yb = _na_call(qn, kn, vn, bias, S)
        yc = _attn_call(qd, kd, vdt, group=2, n_kv=2, diff_args=diff_args, lam_init=lam_init, name="diff_attn",
                        **lat(ATTN_TILE // 2))
        yd = _attn_call(qg, kg, vgt, group=2, n_kv=1, name="gqa_attn", **lat(ATTN_TILE))

        wb = w_branch[l].astype(BF16)
        wo = w_out[l].astype(BF16)
        wgu = w_gate_up[l].astype(BF16)
        wd = w_down[l].astype(BF16)
        ln1 = (row(ln1_g[l]), row(ln1_b[l]))
        ln2 = (row(ln2_g[l]), row(ln2_b[l]))

        if need_ctx:
            cx = dict(tq=CTX_LEN, q_blk0=ctx_blk, n_q=1, t_blk=CTX_LEN, k_blk0=ctx_blk)
            ya_c = _attn_call(qa, ka, vat, group=1, n_kv=2, name="mla_attn_ctx", **cx)
            ctx0 = dict(tq=CTX_LEN, q_blk0=0, n_q=1, t_blk=CTX_LEN, k_blk0=0)
            yb_c = _attn_call(qn[:, :, S:], kn[:, :, S:], _transpose_v(vn[:, :, S:]), group=1, n_kv=2,
                              name="na_attn_ctx", **ctx0)
            yc_c = _attn_call(qd, kd, vdt, group=2, n_kv=2, diff_args=diff_args, lam_init=lam_init,
                              name="diff_attn_ctx", **cx)
            yd_c = _attn_call(qg, kg, vgt, group=2, n_kv=1, name="gqa_attn_ctx", **cx)
            xc1 = _merge_call(xc, mod_c, (ya_c, yb_c, yc_c, yd_c), wg, wb, wo, *ln1, tm=CTX_LEN)
            xc = _ffn_call(xc1, mod_c, wgu, wd, *ln2, tm=CTX_LEN)

        xl1 = _merge_call(xl, mod_l, (ya, yb, yc, yd), wg, wb, wo, *ln1, tm=Q_TILE)
        xl = _ffn_call(xl1, mod_l, wgu, wd, *ln2, tm=Q_TILE)
    return xl
```

```python
import functools
import math

import jax
import jax.numpy as jnp
import numpy as np
from jax import lax
from jax.experimental import pallas as pl
from jax.experimental.pallas import tpu as pltpu

F32 = jnp.float32
BF16 = jnp.bfloat16

D_MODEL = 1024
DEPTH = 2
CTX_LEN = 256
GRID_W = 64
ROPE_THETA = 10000.0
EPS = 1e-6

MLA_HEADS = 4
MLA_Q_RANK = 384
MLA_KV_RANK = 256
MLA_NOPE = 64
MLA_ROPE = 32
MLA_V = 64
NA_HEADS = 4
NA_DIM = 64
NA_ROWS = 8
NA_COLS = 16
DIFF_HEADS = 4
DIFF_QK = 32
DIFF_V = 64
GQA_HEADS = 4
GQA_KV_HEADS = 2
GQA_DIM = 64
N_BRANCH = 4
BRANCH_W = 256
D_FF = -(-8 * D_MODEL // (3 * 256)) * 256
DEEPNORM_ALPHA = (2 * DEPTH) ** 0.25

IN_SIZES = (
    MLA_Q_RANK, MLA_KV_RANK, MLA_ROPE,
    NA_HEADS * NA_DIM, NA_HEADS * NA_DIM, NA_HEADS * NA_DIM,
    DIFF_HEADS * 2 * DIFF_QK, DIFF_HEADS * 2 * DIFF_QK, DIFF_HEADS * DIFF_V,
    GQA_HEADS * GQA_DIM, GQA_KV_HEADS * GQA_DIM, GQA_KV_HEADS * GQA_DIM,
    N_BRANCH * D_MODEL,
)
IN_OFFS = tuple(int(v) for v in np.concatenate([[0], np.cumsum(IN_SIZES)]))
GATE_OFF = IN_OFFS[12]

LANES = 128
HEAD_PAD = LANES
V_PAD = 2 * LANES
TOK_TILE = 256
Q_TILE = 512
ATTN_TILE = 512
NA_GROUP_ROWS = 4
NA_WIN_ROWS = 12
SCORE_COLS = 256
MAX_CHAINS = 8
KEY_TILE = 256
KEY_CHUNKS = 4
V_DIM = 64
VT_ROWS = 80
MASK_VALUE = -1e30
VMEM_LIMIT = 56 * 1024 * 1024

LOG2E = math.log2(math.e)
MLA_SCALE = (MLA_NOPE + MLA_ROPE) ** -0.5 * LOG2E
NA_SCALE = NA_DIM ** -0.5 * LOG2E
DIFF_SCALE = DIFF_QK ** -0.5 * LOG2E
GQA_SCALE = GQA_DIM ** -0.5 * LOG2E

P_CQ = 0
P_CKV = P_CQ + MLA_Q_RANK
P_KPE = P_CKV + MLA_KV_RANK
P_QN = P_KPE + HEAD_PAD
P_KN = P_QN + NA_HEADS * HEAD_PAD
P_VN = P_KN + NA_HEADS * HEAD_PAD
P_QD = P_VN + NA_HEADS * HEAD_PAD
P_KD = P_QD + DIFF_HEADS * HEAD_PAD
P_VD = P_KD + DIFF_HEADS * HEAD_PAD
P_QG = P_VD + DIFF_HEADS * HEAD_PAD
P_KG = P_QG + GQA_HEADS * HEAD_PAD
P_VG = P_KG + GQA_KV_HEADS * HEAD_PAD
P_END = P_VG + GQA_KV_HEADS * HEAD_PAD

NT_DIMS = (((1,), (1,)), ((), ()))


def _cparams(sem):
    return pltpu.CompilerParams(dimension_semantics=sem, vmem_limit_bytes=VMEM_LIMIT)


def _const_spec(shape):
    nd = len(shape)
    return pl.BlockSpec(shape, lambda *_: (0,) * nd, pipeline_mode=pl.Buffered(1))


def _head_cols(base, n_heads, dup):
    cols = []
    for h in range(n_heads):
        c = list(range(base + 64 * h, base + 64 * h + 64))
        cols += c + (c if dup else [-1] * 64)
    return cols


def _small_cols():
    o = IN_OFFS
    cols = list(range(o[0], o[1])) + list(range(o[1], o[2]))
    cols += [-1] * 64 + list(range(o[2], o[3])) + [-1] * 32
    cols += _head_cols(o[3], NA_HEADS, False) + _head_cols(o[4], NA_HEADS, False) + _head_cols(o[5], NA_HEADS, True)
    cols += _head_cols(o[6], DIFF_HEADS, False) + _head_cols(o[7], DIFF_HEADS, False) + _head_cols(o[8], DIFF_HEADS, True)
    cols += _head_cols(o[9], GQA_HEADS, False) + _head_cols(o[10], GQA_KV_HEADS, False) + _head_cols(o[11], GQA_KV_HEADS, True)
    assert len(cols) == P_END
    return np.asarray(cols, np.int32)


def _gather_cols(w, idx):
    parts, i, n = [], 0, len(idx)
    while i < n:
        j = i + 1
        if idx[i] < 0:
            while j < n and idx[j] < 0:
                j += 1
            parts.append(jnp.zeros((w.shape[0], j - i), w.dtype))
        else:
            while j < n and idx[j] == idx[j - 1] + 1:
                j += 1
            parts.append(w[:, int(idx[i]):int(idx[j - 1]) + 1])
        i = j
    return jnp.concatenate(parts, axis=1)


def _qup_cols():
    cols = []
    for h in range(MLA_HEADS):
        cols += list(range(96 * h, 96 * h + 96)) + [-1] * 32
    return np.asarray(cols, np.int32)


def _kvup_cols():
    kc, vc = [], []
    for h in range(MLA_HEADS):
        kc += list(range(128 * h, 128 * h + 64)) + [-1] * 64
        v = list(range(128 * h + 64, 128 * h + 128))
        vc += v + v
    return np.asarray(kc, np.int32), np.asarray(vc, np.int32)


def _rope_pattern(S, rot_dim):
    t = jnp.arange(S)
    pos = jnp.stack([t // GRID_W, t % GRID_W], axis=-1).astype(F32)
    n_f = rot_dim // 4
    inv = ROPE_THETA ** (-jnp.arange(n_f, dtype=F32) / n_f)
    ang = pos[:, :, None] * inv
    cos, sin = jnp.cos(ang), jnp.sin(ang)
    c = jnp.concatenate([cos[:, 0], cos[:, 0], cos[:, 1], cos[:, 1]], axis=-1)
    s = jnp.concatenate([-sin[:, 0], sin[:, 0], -sin[:, 1], sin[:, 1]], axis=-1)
    c = jnp.concatenate([c, jnp.ones((CTX_LEN, rot_dim), F32)], axis=0)
    s = jnp.concatenate([s, jnp.zeros((CTX_LEN, rot_dim), F32)], axis=0)
    return c, s


def _rope_tables(S):
    T = S + CTX_LEN
    c32, s32 = _rope_pattern(S, 32)
    c64, s64 = _rope_pattern(S, 64)
    one = lambda n: jnp.ones((T, n), F32)
    zero = lambda n: jnp.zeros((T, n), F32)
    cat = lambda xs: jnp.concatenate(xs, axis=-1)
    ca, sa = cat([one(64), c32, one(32)]), cat([zero(64), s32, zero(32)])
    cd, sd = cat([c32, c32, one(64)]), cat([s32, s32, zero(64)])
    cg, sg = cat([c64, one(64)]), cat([s64, zero(64)])
    return ca, sa, cd, sd, cg, sg


def _na_bias_tables(rpb_l, rows):
    n_heads = rpb_l.shape[0]
    n_groups = rows // NA_GROUP_ROWS
    kh, kw = min(NA_ROWS, rows), NA_COLS
    w = np.arange(GRID_W)[:, None]
    c = np.arange(GRID_W)[None, :]
    cs = np.clip(w - kw // 2, 0, GRID_W - kw)
    col_valid = (c >= cs) & (c < cs + kw)
    col_off = c - w + (NA_COLS - 1)
    onehot_c = ((col_off[None] == np.arange(2 * NA_COLS - 1)[:, None, None]) & col_valid[None]).astype(np.float32)
    a = np.arange(NA_GROUP_ROWS)[:, None]
    i = np.arange(NA_WIN_ROWS)[None, :]
    out = []
    for g in (0, 1, n_groups - 1):
        r = NA_GROUP_ROWS * g + a
        i_abs = np.clip(NA_GROUP_ROWS * g - kh // 2, 0, rows - NA_WIN_ROWS) + i
        rs = np.clip(r - kh // 2, 0, rows - kh)
        row_valid = (i_abs >= rs) & (i_abs < rs + kh)
        row_off = i_abs - r + (NA_ROWS - 1)
        onehot_r = ((row_off[..., None] == np.arange(2 * NA_ROWS - 1)) & row_valid[..., None]).astype(np.float32)
        valid = row_valid[:, None, :, None] & col_valid[None, :, None, :]
        b = jnp.einsum('air,hrk,kwc->hawic', jnp.asarray(onehot_r), rpb_l.astype(F32), jnp.asarray(onehot_c),
                       precision=lax.Precision.HIGHEST)
        b = jnp.where(jnp.asarray(valid)[None], b * LOG2E, MASK_VALUE)
        out.append(b.reshape(n_heads, NA_GROUP_ROWS * GRID_W, NA_WIN_ROWS * GRID_W))
    return jnp.stack(out, axis=0)


def _sigmoid(x):
    return 1.0 / (1.0 + jnp.exp(-x))


def _rms(x, g, n):
    ms = jnp.sum(x * x, axis=-1, keepdims=True) * (1.0 / n)
    return x * lax.rsqrt(ms + EPS) * g


def _layer_norm(x, g, b):
    mu = jnp.mean(x, axis=-1, keepdims=True)
    xc = x - mu
    var = jnp.mean(xc * xc, axis=-1, keepdims=True)
    return xc * lax.rsqrt(var + EPS) * g + b


def _rope(z, c, s, n_f):
    lane = lax.broadcasted_iota(jnp.int32, z.shape, 1)
    first = (lane & n_f) == 0
    up = pltpu.roll(z, LANES - n_f, 1)
    dn = pltpu.roll(z, n_f, 1)
    return z * c + jnp.where(first, up, dn) * s


def _bdot(a, b):
    return jnp.dot(a, b, preferred_element_type=F32)


def _ada_kernel(c_ref, w_ref, b_ref, o_ref):
    c = c_ref[...]
    a = c * _sigmoid(c)
    o_ref[0] = jnp.dot(a, w_ref[0], preferred_element_type=F32, precision=lax.Precision.HIGHEST) + b_ref[0]


def _ada_call(c8, w_ada, b_ada):
    L, D, N = w_ada.shape
    tn = 1536
    return pl.pallas_call(
        _ada_kernel,
        grid=(L, N // tn),
        in_specs=[
            pl.BlockSpec((8, D), lambda l, n: (0, 0)),
            pl.BlockSpec((1, D, tn), lambda l, n: (l, 0, n)),
            pl.BlockSpec((1, 1, tn), lambda l, n: (l, 0, n)),
        ],
        out_specs=pl.BlockSpec((1, 8, tn), lambda l, n: (l, 0, n)),
        out_shape=jax.ShapeDtypeStruct((L, 8, N), F32),
        compiler_params=_cparams(("parallel", "parallel")),
        name="ada_mod",
    )(c8, w_ada, b_ada.reshape(L, 1, N))


def _proj_kernel(x_ref, ctx_ref, modl_ref, modc_ref, w_ref, gqa_ref, wqup_ref, gkva_ref, wkvk_ref, wkvv_ref,
                 gqn_ref, gkn_ref, ca_ref, sa_ref, cd_ref, sd_ref, cg_ref, sg_ref,
                 qa_ref, ka_ref, va_ref, qn_ref, kn_ref, vn_ref, qd_ref, kd_ref, vd_ref, qg_ref, kg_ref, vg_ref,
                 *, n_lat_tiles):
    D = D_MODEL
    is_ctx = pl.program_id(0) == n_lat_tiles
    x = jnp.where(is_ctx, ctx_ref[0], x_ref[0])
    ml, mc = modl_ref[0], modc_ref[0]
    sh = jnp.where(is_ctx, mc[:, 0:D], ml[:, 0:D])
    sc = jnp.where(is_ctx, mc[:, D:2 * D], ml[:, D:2 * D])
    h = (x * (1.0 + sc) + sh).astype(BF16)
    z = _bdot(h, w_ref[...])
    tm = z.shape[0]
    ones = jnp.ones((tm, LANES), F32)
    lane = lax.broadcasted_iota(jnp.int32, (tm, LANES), 1)
    head = lambda base, i: z[:, base + HEAD_PAD * i: base + HEAD_PAD * (i + 1)]
    with_ones = lambda v: jnp.concatenate([v, ones], axis=-1).astype(BF16)
    tail = (lax.broadcasted_iota(jnp.int32, (VT_ROWS - V_DIM, tm), 0) == 0).astype(F32)
    transposed = lambda v: jnp.concatenate([v.T[:V_DIM], tail], axis=0).astype(BF16)

    ca, sa = ca_ref[...], sa_ref[...]
    cqn = _rms(z[:, P_CQ:P_CKV], gqa_ref[...], MLA_Q_RANK).astype(BF16)
    qa_full = _bdot(cqn, wqup_ref[...])
    ckvn = _rms(z[:, P_CKV:P_KPE], gkva_ref[...], MLA_KV_RANK).astype(BF16)
    k_full = _bdot(ckvn, wkvk_ref[...])
    v_full = _bdot(ckvn, wkvv_ref[...])
    kpe = _rope(z[:, P_KPE:P_QN], ca, sa, MLA_ROPE // 4)
    for i in range(MLA_HEADS):
        sl = slice(HEAD_PAD * i, HEAD_PAD * (i + 1))
        qa_ref[0, i] = (_rope(qa_full[:, sl], ca, sa, MLA_ROPE // 4) * MLA_SCALE).astype(BF16)
        ka_ref[0, i] = (k_full[:, sl] + kpe).astype(BF16)
        va_ref[0, i] = transposed(v_full[:, sl])

    for i in range(NA_HEADS):
        qn_ref[0, i] = (head(P_QN, i) * NA_SCALE).astype(BF16)
        kn_ref[0, i] = head(P_KN, i).astype(BF16)
        vn_ref[0, i] = with_ones(head(P_VN, i))

    cd, sd = cd_ref[...], sd_ref[...]
    for i in range(DIFF_HEADS):
        zq = _rope(head(P_QD, i), cd, sd, DIFF_QK // 4) * DIFF_SCALE
        qd_ref[0, 2 * i] = jnp.where(lane < DIFF_QK, zq, 0.0).astype(BF16)
        qd_ref[0, 2 * i + 1] = jnp.where(lane >= DIFF_QK, zq, 0.0).astype(BF16)
        kd_ref[0, i] = _rope(head(P_KD, i), cd, sd, DIFF_QK // 4).astype(BF16)
        vd_ref[0, i] = transposed(head(P_VD, i))

    cg, sg = cg_ref[...], sg_ref[...]
    gqn, gkn = gqn_ref[...], gkn_ref[...]
    for i in range(GQA_HEADS):
        qn = _rms(head(P_QG, i), gqn, GQA_DIM)
        qg_ref[0, i] = (_rope(qn, cg, sg, GQA_DIM // 4) * GQA_SCALE).astype(BF16)
    for i in range(GQA_KV_HEADS):
        kn = _rms(head(P_KG, i), gkn, GQA_DIM)
        kg_ref[0, i] = _rope(kn, cg, sg, GQA_DIM // 4).astype(BF16)
        vg_ref[0, i] = transposed(head(P_VG, i))


def _proj_call(xl, xc, mod_l, mod_c, wsm, gqa, wqup, gkva, wkvk, wkvv, gqn, gkn, tabs):
    B, S, D = xl.shape
    tm = TOK_TILE
    n_lat = S // tm
    T = S + CTX_LEN
    grid = (n_lat + 1, B)
    tab_spec = pl.BlockSpec((tm, LANES), lambda t, b: (t, 0))
    in_specs = [
        pl.BlockSpec((1, tm, D), lambda t, b: (b, jnp.minimum(t, n_lat - 1), 0)),
        pl.BlockSpec((1, tm, D), lambda t, b: (b, 0, 0)),
        pl.BlockSpec((1, 1, 6 * D), lambda t, b: (b, 0, 0)),
        pl.BlockSpec((1, 1, 6 * D), lambda t, b: (0, 0, 0)),
        _const_spec(wsm.shape), _const_spec(gqa.shape), _const_spec(wqup.shape), _const_spec(gkva.shape),
        _const_spec(wkvk.shape), _const_spec(wkvv.shape), _const_spec(gqn.shape), _const_spec(gkn.shape),
    ] + [tab_spec] * 6

    def o(n_heads, width):
        return (jax.ShapeDtypeStruct((B, n_heads, T, width), BF16),
                pl.BlockSpec((1, n_heads, tm, width), lambda t, b: (b, 0, t, 0)))

    def ot(n_heads):
        return (jax.ShapeDtypeStruct((B, n_heads, VT_ROWS, T), BF16),
                pl.BlockSpec((1, n_heads, VT_ROWS, tm), lambda t, b: (b, 0, 0, t)))

    outs = [o(MLA_HEADS, HEAD_PAD), o(MLA_HEADS, HEAD_PAD), ot(MLA_HEADS),
            o(NA_HEADS, HEAD_PAD), o(NA_HEADS, HEAD_PAD), o(NA_HEADS, V_PAD),
            o(2 * DIFF_HEADS, HEAD_PAD), o(DIFF_HEADS, HEAD_PAD), ot(DIFF_HEADS),
            o(GQA_HEADS, HEAD_PAD), o(GQA_KV_HEADS, HEAD_PAD), ot(GQA_KV_HEADS)]
    return pl.pallas_call(
        functools.partial(_proj_kernel, n_lat_tiles=n_lat),
        grid=grid,
        in_specs=in_specs,
        out_specs=[s for _, s in outs],
        out_shape=[s for s, _ in outs],
        compiler_params=_cparams(("arbitrary", "arbitrary")),
        name="in_proj",
    )(xl, xc, mod_l, mod_c, wsm, gqa, wqup, gkva, wkvk, wkvv, gqn, gkn, *tabs)


def _key_chunks(t_keys):
    n_tiles = t_keys // KEY_TILE
    n_chunks = min(KEY_CHUNKS, n_tiles)
    sizes = [n_tiles // n_chunks + (1 if c < n_tiles % n_chunks else 0) for c in range(n_chunks)]
    bounds = np.cumsum([0] + sizes) * KEY_TILE
    return [(int(bounds[c]), int(bounds[c + 1])) for c in range(n_chunks)]


def _attn_kernel(*refs, n_kv, group, diff, lam_init):
    q_ref, k_ref, v_ref = refs[:3]
    s_ref, ma_ref, mb_ref, pa_ref, pb_ref = refs[-5:]
    step = pl.program_id(0)
    tq = q_ref.shape[2]
    n_dots = s_ref.shape[0]
    cols = s_ref.shape[2]
    split = n_dots // n_kv

    @pl.when(step == 0)
    def _():
        s_ref[...] = jnp.zeros(s_ref.shape, F32)
        mb_ref[...] = jnp.zeros(mb_ref.shape, F32)
        pb_ref[...] = jnp.ones(pb_ref.shape, BF16)

    def body(m_write, m_read, p_write, p_read):
        qs = []
        for j in range(n_kv):
            q = q_ref[0, j * group:(j + 1) * group].reshape(group * tq, HEAD_PAD)
            qs += [q[r * cols:(r + 1) * cols] for r in range(split)]
        acc = [None] * n_dots
        run_max = [None] * n_dots
        for lo, hi in _key_chunks(k_ref.shape[2]):
            for d in range(n_dots):
                j = d // split
                part = _bdot(v_ref[0, j, :, lo:hi], p_read[d, lo:hi, :])
                acc[d] = part if acc[d] is None else acc[d] + part
                p_write[d, lo:hi, :] = jnp.exp2(s_ref[d, lo:hi, :] - m_read[d]).astype(BF16)
                s = lax.dot_general(k_ref[0, j, lo:hi, :], qs[d], NT_DIMS, preferred_element_type=F32)
                s_ref[d, lo:hi, :] = s
                pm = jnp.max(s.reshape(MAX_CHAINS, (hi - lo) // MAX_CHAINS, cols), axis=1)
                run_max[d] = pm if run_max[d] is None else jnp.maximum(run_max[d], pm)
        heads = []
        for j in range(n_kv):
            outs = []
            for d in range(j * split, (j + 1) * split):
                m_write[d] = jnp.max(run_max[d], axis=0, keepdims=True)
                outs.append(acc[d][:V_DIM] / acc[d][V_DIM:V_DIM + 1])
            on = outs[0] if split == 1 else jnp.concatenate(outs, axis=1)
            heads += [on[:, g * tq:(g + 1) * tq] for g in range(group)]
        _attn_finish(heads, refs, diff, lam_init)

    parity = lax.rem(step, 2)

    @pl.when(parity == 0)
    def _():
        body(ma_ref, mb_ref, pa_ref, pb_ref)

    @pl.when(parity == 1)
    def _():
        body(mb_ref, ma_ref, pb_ref, pa_ref)


def _attn_finish(heads, refs, diff, lam_init):
    if diff:
        lq1_ref, lk1_ref, lq2_ref, lk2_ref, gsub_ref, o_ref = refs[3:9]
    else:
        o_ref = refs[3]
    if diff:
        ys = []
        for j in range(2):
            dot_l = lambda a, b: jnp.sum(a[0, j:j + 1] * b[0, j:j + 1], axis=-1, keepdims=True)
            lam = jnp.exp(dot_l(lq1_ref, lk1_ref)) - jnp.exp(dot_l(lq2_ref, lk2_ref)) + lam_init
            y = heads[2 * j] - lam * heads[2 * j + 1]
            ms = jnp.mean(y * y, axis=0, keepdims=True)
            ys.append(y * lax.rsqrt(ms + EPS) * gsub_ref[...] * (1.0 - lam_init))
        heads = ys
    o_ref[0] = jnp.concatenate(heads, axis=0).T.astype(BF16)


def _transpose_v(v):
    B, H, T, _ = v.shape
    vt = jnp.swapaxes(v[..., :V_DIM], -1, -2)
    tail = jnp.zeros((B, H, VT_ROWS - V_DIM, T), v.dtype).at[:, :, 0, :].set(1.0)
    return jnp.concatenate([vt, tail], axis=2)


def _attn_call(q, k, vt, *, group, n_kv, tq, q_blk0, n_q, t_blk, k_blk0, diff_args=None, lam_init=0.0, name):
    B, Hq = q.shape[0], q.shape[1]
    n_pairs = Hq // (n_kv * group)
    qh = n_kv * group
    n_units = B * n_pairs * n_q
    n_dots = qh * tq // SCORE_COLS

    def unit(u):
        return u // (n_pairs * n_q), (u // n_q) % n_pairs, u % n_q

    def stage1(s):
        return unit(jnp.minimum(s, n_units - 1))

    def stage3(s):
        return unit(jnp.maximum(s - 2, 0))

    def q_map(s):
        b, p, i = stage1(s)
        return b, p, q_blk0 + i, 0

    def k_map(s):
        b, p, _ = stage1(s)
        return b, p, k_blk0, 0

    def v_map(s):
        b, p, _ = stage3(s)
        return b, p, 0, k_blk0

    def o_map(s):
        b, p, i = stage3(s)
        return b, i, p

    in_specs = [
        pl.BlockSpec((1, qh, tq, HEAD_PAD), q_map),
        pl.BlockSpec((1, n_kv, t_blk, HEAD_PAD), k_map),
        pl.BlockSpec((1, n_kv, VT_ROWS, t_blk), v_map),
    ]
    args = [q, k, vt]
    if diff_args is not None:
        lam_spec = pl.BlockSpec((1, 2, DIFF_QK), lambda s: (stage3(s)[1], 0, 0))
        in_specs += [lam_spec] * 4 + [pl.BlockSpec((V_DIM, 1), lambda s: (0, 0))]
        args += list(diff_args)
    score_buf = pltpu.VMEM((n_dots, t_blk, SCORE_COLS), F32)
    prob_buf = pltpu.VMEM((n_dots, t_blk, SCORE_COLS), BF16)
    max_buf = pltpu.VMEM((n_dots, 1, SCORE_COLS), F32)
    return pl.pallas_call(
        functools.partial(_attn_kernel, n_kv=n_kv, group=group, diff=diff_args is not None, lam_init=lam_init),
        grid=(n_units + 2,),
        in_specs=in_specs,
        out_specs=pl.BlockSpec((1, tq, LANES), o_map),
        out_shape=jax.ShapeDtypeStruct((B, n_q * tq, n_pairs * LANES), BF16),
        scratch_shapes=[score_buf, max_buf, max_buf, prob_buf, prob_buf],
        compiler_params=_cparams(("arbitrary",)),
        name=name,
    )(*args)


def _na_kernel(q_ref, k_ref, v_ref, bias_ref, o_ref, *, n_lat, win):
    g = pl.program_id(1)
    rows = n_lat // GRID_W
    start = jnp.clip(NA_GROUP_ROWS * g - NA_ROWS // 2, 0, rows - NA_WIN_ROWS) * GRID_W
    start = pl.multiple_of(start, GRID_W)
    tq = q_ref.shape[2]
    lane = lax.broadcasted_iota(jnp.int32, (tq, LANES), 1)
    outs = []
    for h in range(NA_HEADS):
        q = q_ref[0, h]
        s_l = lax.dot_general(q, k_ref[0, h, pl.ds(start, win), :], NT_DIMS, preferred_element_type=F32)
        s_l = s_l + bias_ref[0, h]
        s_c = lax.dot_general(q, k_ref[0, h, n_lat:, :], NT_DIMS, preferred_element_type=F32)
        m = jnp.maximum(jnp.max(s_l, axis=-1, keepdims=True), jnp.max(s_c, axis=-1, keepdims=True))
        p_l = jnp.exp2(s_l - m).astype(BF16)
        p_c = jnp.exp2(s_c - m).astype(BF16)
        oe = _bdot(p_l, v_ref[0, h, pl.ds(start, win), :]) + _bdot(p_c, v_ref[0, h, n_lat:, :])
        outs.append(oe[:, :LANES] / oe[:, LANES:])
    out = jnp.concatenate([jnp.where(lane < 64, outs[0], outs[1]), jnp.where(lane < 64, outs[2], outs[3])], axis=-1)
    o_ref[0] = out.astype(BF16)


def _na_call(q, k, v, bias, n_lat):
    B, H, T, _ = q.shape
    tq = NA_GROUP_ROWS * GRID_W
    win = NA_WIN_ROWS * GRID_W
    n_groups = n_lat // tq

    def variant(g):
        return jnp.where(g == 0, 0, jnp.where(g == n_groups - 1, 2, 1))

    return pl.pallas_call(
        functools.partial(_na_kernel, n_lat=n_lat, win=win),
        grid=(B, n_groups),
        in_specs=[
            pl.BlockSpec((1, H, tq, HEAD_PAD), lambda b, g: (b, 0, g, 0)),
            pl.BlockSpec((1, H, T, HEAD_PAD), lambda b, g: (b, 0, 0, 0)),
            pl.BlockSpec((1, H, T, V_PAD), lambda b, g: (b, 0, 0, 0)),
            pl.BlockSpec((1, H, tq, win), lambda b, g: (variant(g), 0, 0, 0)),
        ],
        out_specs=pl.BlockSpec((1, tq, H * 64), lambda b, g: (b, g, 0)),
        out_shape=jax.ShapeDtypeStruct((B, n_lat, H * 64), BF16),
        compiler_params=_cparams(("parallel", "arbitrary")),
        name="na_attn",
    )(q, k, v, bias)


def _merge_kernel(x_ref, mod_ref, ya_ref, yb_ref, yc_ref, yd_ref, wg_ref, wb_ref, wo_ref, lg_ref, lb_ref, o_ref):
    D = D_MODEL
    x = x_ref[0]
    mod = mod_ref[0]
    sh, sc, gate_res = mod[:, 0:D], mod[:, D:2 * D], mod[:, 2 * D:3 * D]
    h = (x * (1.0 + sc) + sh).astype(BF16)
    acc = None
    for i, y_ref in enumerate((ya_ref, yb_ref, yc_ref, yd_ref)):
        gate = _sigmoid(_bdot(h, wg_ref[:, i * D:(i + 1) * D]))
        term = gate * _bdot(y_ref[0], wb_ref[i])
        acc = term if acc is None else acc + term
    out = _bdot(acc.astype(BF16), wo_ref[...])
    o_ref[0] = _layer_norm(DEEPNORM_ALPHA * x + gate_res * out, lg_ref[...], lb_ref[...])


def _merge_call(x, mod, ys, wg, wb, wo, lg, lb, tm):
    B, S, D = x.shape
    per_batch_mod = mod.shape[0] > 1
    tok = lambda w: pl.BlockSpec((1, tm, w), lambda b, i: (b, i, 0))
    return pl.pallas_call(
        _merge_kernel,
        grid=(B, S // tm),
        in_specs=[tok(D), pl.BlockSpec((1, 1, 6 * D), lambda b, i: (b if per_batch_mod else 0, 0, 0))]
        + [tok(BRANCH_W)] * 4
        + [_const_spec(wg.shape), _const_spec(wb.shape), _const_spec(wo.shape), _const_spec(lg.shape), _const_spec(lb.shape)],
        out_specs=tok(D),
        out_shape=jax.ShapeDtypeStruct((B, S, D), F32),
        compiler_params=_cparams(("parallel", "arbitrary")),
        name="merge_ln",
    )(x, mod, *ys, wg, wb, wo, lg, lb)


def _ffn_kernel(x_ref, mod_ref, wgu_ref, wd_ref, lg_ref, lb_ref, o_ref):
    D = D_MODEL
    x = x_ref[0]
    mod = mod_ref[0]
    sh, sc, gate_res = mod[:, 3 * D:4 * D], mod[:, 4 * D:5 * D], mod[:, 5 * D:6 * D]
    h = (x * (1.0 + sc) + sh).astype(BF16)
    gu = _bdot(h, wgu_ref[...])
    g, u = gu[:, :D_FF], gu[:, D_FF:]
    act = (g * _sigmoid(g) * u).astype(BF16)
    down = _bdot(act, wd_ref[...])
    o_ref[0] = _layer_norm(DEEPNORM_ALPHA * x + gate_res * down, lg_ref[...], lb_ref[...])


def _ffn_call(x, mod, wgu, wd, lg, lb, tm):
    B, S, D = x.shape
    per_batch_mod = mod.shape[0] > 1
    tok = pl.BlockSpec((1, tm, D), lambda b, i: (b, i, 0))
    return pl.pallas_call(
        _ffn_kernel,
        grid=(B, S // tm),
        in_specs=[tok, pl.BlockSpec((1, 1, 6 * D), lambda b, i: (b if per_batch_mod else 0, 0, 0)),
                  _const_spec(wgu.shape), _const_spec(wd.shape), _const_spec(lg.shape), _const_spec(lb.shape)],
        out_specs=tok,
        out_shape=jax.ShapeDtypeStruct((B, S, D), F32),
        compiler_params=_cparams(("parallel", "arbitrary")),
        name="ffn_ln",
    )(x, mod, wgu, wd, lg, lb)


def kernel(x, c, ctx, c_ctx, w_ada, b_ada, w_in, g_q_a, w_q_up, g_kv_a, w_kv_up, rpb, lam_q1, lam_k1, lam_q2, lam_k2,
           g_sub, g_qn, g_kn, w_branch, w_out, ln1_g, ln1_b, w_gate_up, w_down, ln2_g, ln2_b):
    B, S, D = x.shape
    rows = S // GRID_W
    n_q = S // Q_TILE
    T = S + CTX_LEN
    ctx_blk = S // CTX_LEN

    c8 = jnp.concatenate([c, c_ctx[None, :], jnp.zeros((8 - B - 1, D), F32)], axis=0)
    mod = _ada_call(c8, w_ada, b_ada)
    tabs = _rope_tables(S)
    small_cols, qup_cols = _small_cols(), _qup_cols()
    kvk_cols, kvv_cols = _kvup_cols()
    row = lambda v: v.reshape(1, -1)
    pad64 = lambda v: jnp.concatenate([v, jnp.zeros_like(v)]).reshape(1, LANES)

    xl, xc = x, ctx
    for l in range(DEPTH):
        need_ctx = l < DEPTH - 1
        lam_init = 0.8 - 0.6 * math.exp(-0.3 * l)
        mod_l = mod[l, :B].reshape(B, 1, 6 * D)
        mod_c = mod[l, B:B + 1].reshape(1, 1, 6 * D)
        wsm = _gather_cols(w_in[l], small_cols).astype(BF16)
        wg = w_in[l][:, GATE_OFF:].astype(BF16)
        wqup = _gather_cols(w_q_up[l], qup_cols).astype(BF16)
        wkvk = _gather_cols(w_kv_up[l], kvk_cols).astype(BF16)
        wkvv = _gather_cols(w_kv_up[l], kvv_cols).astype(BF16)

        (qa, ka, vat, qn, kn, vn, qd, kd, vdt, qg, kg, vgt) = _proj_call(
            xl, xc, mod_l, mod_c, wsm, row(g_q_a[l]), wqup, row(g_kv_a[l]), wkvk, wkvv,
            pad64(g_qn[l]), pad64(g_kn[l]), tabs)

        diff_args = (lam_q1[l].reshape(2, 2, DIFF_QK), lam_k1[l].reshape(2, 2, DIFF_QK),
                     lam_q2[l].reshape(2, 2, DIFF_QK), lam_k2[l].reshape(2, 2, DIFF_QK),
                     g_sub[l].reshape(V_DIM, 1))
        bias = _na_bias_tables(rpb[l], rows)

        lat = lambda tq: dict(tq=tq, q_blk0=0, n_q=S // tq, t_blk=T, k_blk0=0)
        ya = _attn_call(qa, ka, vat, group=1, n_kv=2, name="mla_attn", **lat(ATTN_TILE))
        yb = _na_call(qn, kn, vn, bias, S)
        yc = _attn_call(qd, kd, vdt, group=2, n_kv=2, diff_args=diff_args, lam_init=lam_init, name="diff_attn",
                        **lat(ATTN_TILE // 2))
        yd = _attn_call(qg, kg, vgt, group=2, n_kv=1, name="gqa_attn", **lat(ATTN_TILE))

        wb = w_branch[l].astype(BF16)
        wo = w_out[l].astype(BF16)
        wgu = w_gate_up[l].astype(BF16)
        wd = w_down[l].astype(BF16)
        ln1 = (row(ln1_g[l]), row(ln1_b[l]))
        ln2 = (row(ln2_g[l]), row(ln2_b[l]))

        if need_ctx:
            cx = dict(tq=CTX_LEN, q_blk0=ctx_blk, n_q=1, t_blk=CTX_LEN, k_blk0=ctx_blk)
            ya_c = _attn_call(qa, ka, vat, group=1, n_kv=2, name="mla_attn_ctx", **cx)
            ctx0 = dict(tq=CTX_LEN, q_blk0=0, n_q=1, t_blk=CTX_LEN, k_blk0=0)
            yb_c = _attn_call(qn[:, :, S:], kn[:, :, S:], _transpose_v(vn[:, :, S:]), group=1, n_kv=2,
                              name="na_attn_ctx", **ctx0)
            yc_c = _attn_call(qd, kd, vdt, group=2, n_kv=2, diff_args=diff_args, lam_init=lam_init,
                              name="diff_attn_ctx", **cx)
            yd_c = _attn_call(qg, kg, vgt, group=2, n_kv=1, name="gqa_attn_ctx", **cx)
            xc1 = _merge_call(xc, mod_c, (ya_c, yb_c, yc_c, yd_c), wg, wb, wo, *ln1, tm=CTX_LEN)
            xc = _ffn_call(xc1, mod_c, wgu, wd, *ln2, tm=CTX_LEN)

        xl1 = _merge_call(xl, mod_l, (ya, yb, yc, yd), wg, wb, wo, *ln1, tm=Q_TILE)
        xl = _ffn_call(xl1, mod_l, wgu, wd, *ln2, tm=Q_TILE)
    return xl
```

```python
import functools
import math

import jax
import jax.numpy as jnp
import numpy as np
from jax import lax
from jax.experimental import pallas as pl
from jax.experimental.pallas import tpu as pltpu

F32 = jnp.float32
BF16 = jnp.bfloat16

D_MODEL = 1024
DEPTH = 2
CTX_LEN = 256
GRID_W = 64
ROPE_THETA = 10000.0
EPS = 1e-6

MLA_HEADS = 4
MLA_Q_RANK = 384
MLA_KV_RANK = 256
MLA_NOPE = 64
MLA_ROPE = 32
MLA_V = 64
NA_HEADS = 4
NA_DIM = 64
NA_ROWS = 8
NA_COLS = 16
DIFF_HEADS = 4
DIFF_QK = 32
DIFF_V = 64
GQA_HEADS = 4
GQA_KV_HEADS = 2
GQA_DIM = 64
N_BRANCH = 4
BRANCH_W = 256
D_FF = -(-8 * D_MODEL // (3 * 256)) * 256
DEEPNORM_ALPHA = (2 * DEPTH) ** 0.25

IN_SIZES = (
    MLA_Q_RANK, MLA_KV_RANK, MLA_ROPE,
    NA_HEADS * NA_DIM, NA_HEADS * NA_DIM, NA_HEADS * NA_DIM,
    DIFF_HEADS * 2 * DIFF_QK, DIFF_HEADS * 2 * DIFF_QK, DIFF_HEADS * DIFF_V,
    GQA_HEADS * GQA_DIM, GQA_KV_HEADS * GQA_DIM, GQA_KV_HEADS * GQA_DIM,
    N_BRANCH * D_MODEL,
)
IN_OFFS = tuple(int(v) for v in np.concatenate([[0], np.cumsum(IN_SIZES)]))
GATE_OFF = IN_OFFS[12]

LANES = 128
HEAD_PAD = LANES
V_PAD = 2 * LANES
TOK_TILE = 256
Q_TILE = 512
ATTN_TILE = 512
NA_GROUP_ROWS = 4
NA_WIN_ROWS = 12
SCORE_COLS = 256
MAX_CHAINS = 8
KEY_TILE = 256
KEY_CHUNKS = 4
V_DIM = 64
VT_ROWS = 80
MASK_VALUE = -1e30
VMEM_LIMIT = 56 * 1024 * 1024

LOG2E = math.log2(math.e)
MLA_SCALE = (MLA_NOPE + MLA_ROPE) ** -0.5 * LOG2E
NA_SCALE = NA_DIM ** -0.5 * LOG2E
DIFF_SCALE = DIFF_QK ** -0.5 * LOG2E
GQA_SCALE = GQA_DIM ** -0.5 * LOG2E

P_CQ = 0
P_CKV = P_CQ + MLA_Q_RANK
P_KPE = P_CKV + MLA_KV_RANK
P_QN = P_KPE + HEAD_PAD
P_KN = P_QN + NA_HEADS * HEAD_PAD
P_VN = P_KN + NA_HEADS * HEAD_PAD
P_QD = P_VN + NA_HEADS * HEAD_PAD
P_KD = P_QD + DIFF_HEADS * HEAD_PAD
P_VD = P_KD + DIFF_HEADS * HEAD_PAD
P_QG = P_VD + DIFF_HEADS * HEAD_PAD
P_KG = P_QG + GQA_HEADS * HEAD_PAD
P_VG = P_KG + GQA_KV_HEADS * HEAD_PAD
P_END = P_VG + GQA_KV_HEADS * HEAD_PAD

NT_DIMS = (((1,), (1,)), ((), ()))


def _cparams(sem):
    return pltpu.CompilerParams(dimension_semantics=sem, vmem_limit_bytes=VMEM_LIMIT)


def _const_spec(shape):
    nd = len(shape)
    return pl.BlockSpec(shape, lambda *_: (0,) * nd, pipeline_mode=pl.Buffered(1))


def _layer_spec(w, layer):
    nd = w.ndim - 1
    return pl.BlockSpec((1,) + w.shape[1:], lambda *_: (layer,) + (0,) * nd, pipeline_mode=pl.Buffered(1))


def _pad_heads(w, n_heads, width, dup=False):
    lead = w.shape[:-1]
    h = w.reshape(lead + (n_heads, width))
    fill = h if dup else jnp.zeros(lead + (n_heads, HEAD_PAD - width), w.dtype)
    return jnp.concatenate([h, fill], axis=-1).reshape(lead + (n_heads * HEAD_PAD,))


def _small_weights(w_in):
    o = IN_OFFS
    seg = lambda i: w_in[..., o[i]:o[i + 1]]
    zeros = lambda n: jnp.zeros(w_in.shape[:-1] + (n,), w_in.dtype)
    parts = [seg(0), seg(1), zeros(MLA_NOPE), seg(2), zeros(HEAD_PAD - MLA_NOPE - MLA_ROPE),
             _pad_heads(seg(3), NA_HEADS, 64), _pad_heads(seg(4), NA_HEADS, 64), _pad_heads(seg(5), NA_HEADS, 64, True),
             _pad_heads(seg(6), DIFF_HEADS, 64), _pad_heads(seg(7), DIFF_HEADS, 64), _pad_heads(seg(8), DIFF_HEADS, 64, True),
             _pad_heads(seg(9), GQA_HEADS, 64), _pad_heads(seg(10), GQA_KV_HEADS, 64),
             _pad_heads(seg(11), GQA_KV_HEADS, 64, True)]
    out = jnp.concatenate(parts, axis=-1)
    assert out.shape[-1] == P_END
    return out.astype(BF16)


def _kv_up_weights(w_kv_up):
    lead = w_kv_up.shape[:-1]
    h = w_kv_up.reshape(lead + (MLA_HEADS, MLA_NOPE + MLA_V))
    k = _pad_heads(h[..., :MLA_NOPE].reshape(lead + (-1,)), MLA_HEADS, MLA_NOPE)
    v = _pad_heads(h[..., MLA_NOPE:].reshape(lead + (-1,)), MLA_HEADS, MLA_V, True)
    return k.astype(BF16), v.astype(BF16)


def _rope_pattern(S, rot_dim):
    t = jnp.arange(S)
    pos = jnp.stack([t // GRID_W, t % GRID_W], axis=-1).astype(F32)
    n_f = rot_dim // 4
    inv = ROPE_THETA ** (-jnp.arange(n_f, dtype=F32) / n_f)
    ang = pos[:, :, None] * inv
    cos, sin = jnp.cos(ang), jnp.sin(ang)
    c = jnp.concatenate([cos[:, 0], cos[:, 0], cos[:, 1], cos[:, 1]], axis=-1)
    s = jnp.concatenate([-sin[:, 0], sin[:, 0], -sin[:, 1], sin[:, 1]], axis=-1)
    c = jnp.concatenate([c, jnp.ones((CTX_LEN, rot_dim), F32)], axis=0)
    s = jnp.concatenate([s, jnp.zeros((CTX_LEN, rot_dim), F32)], axis=0)
    return c, s


def _rope_tables(S):
    T = S + CTX_LEN
    c32, s32 = _rope_pattern(S, 32)
    c64, s64 = _rope_pattern(S, 64)
    one = lambda n: jnp.ones((T, n), F32)
    zero = lambda n: jnp.zeros((T, n), F32)
    cat = lambda xs: jnp.concatenate(xs, axis=-1)
    ca, sa = cat([one(64), c32, one(32)]), cat([zero(64), s32, zero(32)])
    cd, sd = cat([c32, c32, one(64)]), cat([s32, s32, zero(64)])
    cg, sg = cat([c64, one(64)]), cat([s64, zero(64)])
    return ca, sa, cd, sd, cg, sg


def _na_bias_tables(rpb, rows):
    depth, n_heads = rpb.shape[:2]
    n_groups = rows // NA_GROUP_ROWS
    kh, kw = min(NA_ROWS, rows), NA_COLS
    w = np.arange(GRID_W)[:, None]
    c = np.arange(GRID_W)[None, :]
    cs = np.clip(w - kw // 2, 0, GRID_W - kw)
    col_valid = (c >= cs) & (c < cs + kw)
    col_off = c - w + (NA_COLS - 1)
    onehot_c = ((col_off[None] == np.arange(2 * NA_COLS - 1)[:, None, None]) & col_valid[None]).astype(np.float32)
    a = np.arange(NA_GROUP_ROWS)[:, None]
    i = np.arange(NA_WIN_ROWS)[None, :]
    out = []
    for g in (0, 1, n_groups - 1):
        r = NA_GROUP_ROWS * g + a
        i_abs = np.clip(NA_GROUP_ROWS * g - kh // 2, 0, rows - NA_WIN_ROWS) + i
        rs = np.clip(r - kh // 2, 0, rows - kh)
        row_valid = (i_abs >= rs) & (i_abs < rs + kh)
        row_off = i_abs - r + (NA_ROWS - 1)
        onehot_r = ((row_off[..., None] == np.arange(2 * NA_ROWS - 1)) & row_valid[..., None]).astype(np.float32)
        valid = row_valid[:, None, :, None] & col_valid[None, :, None, :]
        b = jnp.einsum('air,lhrk,kwc->lhawic', jnp.asarray(onehot_r), rpb.astype(F32), jnp.asarray(onehot_c),
                       precision=lax.Precision.HIGHEST)
        b = jnp.where(jnp.asarray(valid)[None, None], b * LOG2E, MASK_VALUE)
        out.append(b.reshape(depth, n_heads, NA_GROUP_ROWS * GRID_W, NA_WIN_ROWS * GRID_W))
    return jnp.stack(out, axis=1).reshape((depth * 3,) + out[0].shape[1:])


def _sigmoid(x):
    return 1.0 / (1.0 + jnp.exp(-x))


def _rms(x, g, n):
    ms = jnp.sum(x * x, axis=-1, keepdims=True) * (1.0 / n)
    return x * lax.rsqrt(ms + EPS) * g


def _layer_norm(x, g, b):
    mu = jnp.mean(x, axis=-1, keepdims=True)
    xc = x - mu
    var = jnp.mean(xc * xc, axis=-1, keepdims=True)
    return xc * lax.rsqrt(var + EPS) * g + b


def _rope(z, c, s, n_f):
    lane = lax.broadcasted_iota(jnp.int32, z.shape, 1)
    first = (lane & n_f) == 0
    up = pltpu.roll(z, LANES - n_f, 1)
    dn = pltpu.roll(z, n_f, 1)
    return z * c + jnp.where(first, up, dn) * s


def _bdot(a, b):
    return jnp.dot(a, b, preferred_element_type=F32)


def _ada_kernel(c_ref, w_ref, b_ref, o_ref):
    c = c_ref[...]
    a = c * _sigmoid(c)
    o_ref[0] = jnp.dot(a, w_ref[0], preferred_element_type=F32, precision=lax.Precision.HIGHEST) + b_ref[0]


def _ada_call(c8, w_ada, b_ada):
    L, D, N = w_ada.shape
    tn = 1536
    return pl.pallas_call(
        _ada_kernel,
        grid=(L, N // tn),
        in_specs=[
            pl.BlockSpec((8, D), lambda l, n: (0, 0)),
            pl.BlockSpec((1, D, tn), lambda l, n: (l, 0, n)),
            pl.BlockSpec((1, 1, tn), lambda l, n: (l, 0, n)),
        ],
        out_specs=pl.BlockSpec((1, 8, tn), lambda l, n: (l, 0, n)),
        out_shape=jax.ShapeDtypeStruct((L, 8, N), F32),
        compiler_params=_cparams(("parallel", "parallel")),
        name="ada_mod",
    )(c8, w_ada, b_ada.reshape(L, 1, N))


def _proj_kernel(x_ref, ctx_ref, modl_ref, modc_ref, w_ref, gqa_ref, wqup_ref, gkva_ref, wkvk_ref, wkvv_ref,
                 gqn_ref, gkn_ref, ca_ref, sa_ref, cd_ref, sd_ref, cg_ref, sg_ref,
                 qa_ref, ka_ref, va_ref, qn_ref, kn_ref, vn_ref, qd_ref, kd_ref, vd_ref, qg_ref, kg_ref, vg_ref,
                 *, n_lat_tiles):
    D = D_MODEL
    is_ctx = pl.program_id(0) == n_lat_tiles
    x = jnp.where(is_ctx, ctx_ref[0], x_ref[0])
    ml, mc = modl_ref[0], modc_ref[0]
    sh = jnp.where(is_ctx, mc[:, 0:D], ml[:, 0:D])
    sc = jnp.where(is_ctx, mc[:, D:2 * D], ml[:, D:2 * D])
    h = (x * (1.0 + sc) + sh).astype(BF16)
    z = _bdot(h, w_ref[0])
    tm = z.shape[0]
    ones = jnp.ones((tm, LANES), F32)
    lane = lax.broadcasted_iota(jnp.int32, (tm, LANES), 1)
    head = lambda base, i: z[:, base + HEAD_PAD * i: base + HEAD_PAD * (i + 1)]
    with_ones = lambda v: jnp.concatenate([v, ones], axis=-1).astype(BF16)
    tail = (lax.broadcasted_iota(jnp.int32, (VT_ROWS - V_DIM, tm), 0) == 0).astype(F32)
    transposed = lambda v: jnp.concatenate([v.T[:V_DIM], tail], axis=0).astype(BF16)

    ca, sa = ca_ref[...], sa_ref[...]
    cqn = _rms(z[:, P_CQ:P_CKV], gqa_ref[...], MLA_Q_RANK).astype(BF16)
    qa_full = _bdot(cqn, wqup_ref[0])
    ckvn = _rms(z[:, P_CKV:P_KPE], gkva_ref[...], MLA_KV_RANK).astype(BF16)
    k_full = _bdot(ckvn, wkvk_ref[0])
    v_full = _bdot(ckvn, wkvv_ref[0])
    kpe = _rope(z[:, P_KPE:P_QN], ca, sa, MLA_ROPE // 4)
    for i in range(MLA_HEADS):
        sl = slice(HEAD_PAD * i, HEAD_PAD * (i + 1))
        qa_ref[0, i] = (_rope(qa_full[:, sl], ca, sa, MLA_ROPE // 4) * MLA_SCALE).astype(BF16)
        ka_ref[0, i] = (k_full[:, sl] + kpe).astype(BF16)
        va_ref[0, i] = transposed(v_full[:, sl])

    for i in range(NA_HEADS):
        qn_ref[0, i] = (head(P_QN, i) * NA_SCALE).astype(BF16)
        kn_ref[0, i] = head(P_KN, i).astype(BF16)
        vn_ref[0, i] = with_ones(head(P_VN, i))

    cd, sd = cd_ref[...], sd_ref[...]
    for i in range(DIFF_HEADS):
        zq = _rope(head(P_QD, i), cd, sd, DIFF_QK // 4) * DIFF_SCALE
        qd_ref[0, 2 * i] = jnp.where(lane < DIFF_QK, zq, 0.0).astype(BF16)
        qd_ref[0, 2 * i + 1] = jnp.where(lane >= DIFF_QK, zq, 0.0).astype(BF16)
        kd_ref[0, i] = _rope(head(P_KD, i), cd, sd, DIFF_QK // 4).astype(BF16)
        vd_ref[0, i] = transposed(head(P_VD, i))

    cg, sg = cg_ref[...], sg_ref[...]
    gqn, gkn = gqn_ref[...], gkn_ref[...]
    for i in range(GQA_HEADS):
        qn = _rms(head(P_QG, i), gqn, GQA_DIM)
        qg_ref[0, i] = (_rope(qn, cg, sg, GQA_DIM // 4) * GQA_SCALE).astype(BF16)
    for i in range(GQA_KV_HEADS):
        kn = _rms(head(P_KG, i), gkn, GQA_DIM)
        kg_ref[0, i] = _rope(kn, cg, sg, GQA_DIM // 4).astype(BF16)
        vg_ref[0, i] = transposed(head(P_VG, i))


def _proj_call(xl, xc, mod_l, mod_c, wsm, gqa, wqup, gkva, wkvk, wkvv, gqn, gkn, tabs, layer):
    B, S, D = xl.shape
    tm = TOK_TILE
    n_lat = S // tm
    T = S + CTX_LEN
    grid = (n_lat + 1, B)
    tab_spec = pl.BlockSpec((tm, LANES), lambda t, b: (t, 0))
    in_specs = [
        pl.BlockSpec((1, tm, D), lambda t, b: (b, jnp.minimum(t, n_lat - 1), 0)),
        pl.BlockSpec((1, tm, D), lambda t, b: (b, 0, 0)),
        pl.BlockSpec((1, 1, 6 * D), lambda t, b: (b, 0, 0)),
        pl.BlockSpec((1, 1, 6 * D), lambda t, b: (0, 0, 0)),
        _layer_spec(wsm, layer), _const_spec(gqa.shape), _layer_spec(wqup, layer), _const_spec(gkva.shape),
        _layer_spec(wkvk, layer), _layer_spec(wkvv, layer), _const_spec(gqn.shape), _const_spec(gkn.shape),
    ] + [tab_spec] * 6

    def o(n_heads, width):
        return (jax.ShapeDtypeStruct((B, n_heads, T, width), BF16),
                pl.BlockSpec((1, n_heads, tm, width), lambda t, b: (b, 0, t, 0)))

    def ot(n_heads):
        return (jax.ShapeDtypeStruct((B, n_heads, VT_ROWS, T), BF16),
                pl.BlockSpec((1, n_heads, VT_ROWS, tm), lambda t, b: (b, 0, 0, t)))

    outs = [o(MLA_HEADS, HEAD_PAD), o(MLA_HEADS, HEAD_PAD), ot(MLA_HEADS),
            o(NA_HEADS, HEAD_PAD), o(NA_HEADS, HEAD_PAD), o(NA_HEADS, V_PAD),
            o(2 * DIFF_HEADS, HEAD_PAD), o(DIFF_HEADS, HEAD_PAD), ot(DIFF_HEADS),
            o(GQA_HEADS, HEAD_PAD), o(GQA_KV_HEADS, HEAD_PAD), ot(GQA_KV_HEADS)]
    return pl.pallas_call(
        functools.partial(_proj_kernel, n_lat_tiles=n_lat),
        grid=grid,
        in_specs=in_specs,
        out_specs=[s for _, s in outs],
        out_shape=[s for s, _ in outs],
        compiler_params=_cparams(("arbitrary", "arbitrary")),
        name="in_proj",
    )(xl, xc, mod_l, mod_c, wsm, gqa, wqup, gkva, wkvk, wkvv, gqn, gkn, *tabs)


def _key_chunks(t_keys):
    n_tiles = t_keys // KEY_TILE
    n_chunks = min(KEY_CHUNKS, n_tiles)
    sizes = [n_tiles // n_chunks + (1 if c < n_tiles % n_chunks else 0) for c in range(n_chunks)]
    bounds = np.cumsum([0] + sizes) * KEY_TILE
    return [(int(bounds[c]), int(bounds[c + 1])) for c in range(n_chunks)]


def _attn_kernel(*refs, n_kv, group, diff, lam_init):
    q_ref, k_ref, v_ref = refs[:3]
    s_ref, ma_ref, mb_ref, pa_ref, pb_ref = refs[-5:]
    step = pl.program_id(0)
    tq = q_ref.shape[2]
    n_dots = s_ref.shape[0]
    cols = s_ref.shape[2]
    split = n_dots // n_kv

    @pl.when(step == 0)
    def _():
        s_ref[...] = jnp.zeros(s_ref.shape, F32)
        mb_ref[...] = jnp.zeros(mb_ref.shape, F32)
        pb_ref[...] = jnp.ones(pb_ref.shape, BF16)

    def body(m_write, m_read, p_write, p_read):
        qs = []
        for j in range(n_kv):
            q = q_ref[0, j * group:(j + 1) * group].reshape(group * tq, HEAD_PAD)
            qs += [q[r * cols:(r + 1) * cols] for r in range(split)]
        acc = [None] * n_dots
        run_max = [None] * n_dots
        for lo, hi in _key_chunks(k_ref.shape[2]):
            for d in range(n_dots):
                j = d // split
                part = _bdot(v_ref[0, j, :, lo:hi], p_read[d, lo:hi, :])
                acc[d] = part if acc[d] is None else acc[d] + part
                p_write[d, lo:hi, :] = jnp.exp2(s_ref[d, lo:hi, :] - m_read[d]).astype(BF16)
                s = lax.dot_general(k_ref[0, j, lo:hi, :], qs[d], NT_DIMS, preferred_element_type=F32)
                s_ref[d, lo:hi, :] = s
                pm = jnp.max(s.reshape(MAX_CHAINS, (hi - lo) // MAX_CHAINS, cols), axis=1)
                run_max[d] = pm if run_max[d] is None else jnp.maximum(run_max[d], pm)
        heads = []
        for j in range(n_kv):
            outs = []
            for d in range(j * split, (j + 1) * split):
                m_write[d] = jnp.max(run_max[d], axis=0, keepdims=True)
                outs.append(acc[d][:V_DIM] / acc[d][V_DIM:V_DIM + 1])
            on = outs[0] if split == 1 else jnp.concatenate(outs, axis=1)
            heads += [on[:, g * tq:(g + 1) * tq] for g in range(group)]
        _attn_finish(heads, refs, diff, lam_init)

    parity = lax.rem(step, 2)

    @pl.when(parity == 0)
    def _():
        body(ma_ref, mb_ref, pa_ref, pb_ref)

    @pl.when(parity == 1)
    def _():
        body(mb_ref, ma_ref, pb_ref, pa_ref)


def _attn_finish(heads, refs, diff, lam_init):
    if diff:
        lq1_ref, lk1_ref, lq2_ref, lk2_ref, gsub_ref, o_ref = refs[3:9]
    else:
        o_ref = refs[3]
    if diff:
        ys = []
        for j in range(2):
            dot_l = lambda a, b: jnp.sum(a[0, j:j + 1] * b[0, j:j + 1], axis=-1, keepdims=True)
            lam = jnp.exp(dot_l(lq1_ref, lk1_ref)) - jnp.exp(dot_l(lq2_ref, lk2_ref)) + lam_init
            y = heads[2 * j] - lam * heads[2 * j + 1]
            ms = jnp.mean(y * y, axis=0, keepdims=True)
            ys.append(y * lax.rsqrt(ms + EPS) * gsub_ref[...] * (1.0 - lam_init))
        heads = ys
    o_ref[0] = jnp.concatenate(heads, axis=0).T.astype(BF16)


def _transpose_v(v):
    B, H, T, _ = v.shape
    vt = jnp.swapaxes(v[..., :V_DIM], -1, -2)
    tail = jnp.zeros((B, H, VT_ROWS - V_DIM, T), v.dtype).at[:, :, 0, :].set(1.0)
    return jnp.concatenate([vt, tail], axis=2)


def _attn_call(q, k, vt, *, group, n_kv, tq, q_blk0, n_q, t_blk, k_blk0, diff_args=None, lam_init=0.0, name):
    B, Hq = q.shape[0], q.shape[1]
    n_pairs = Hq // (n_kv * group)
    qh = n_kv * group
    n_units = B * n_pairs * n_q
    n_dots = qh * tq // SCORE_COLS

    def unit(u):
        return u // (n_pairs * n_q), (u // n_q) % n_pairs, u % n_q

    def stage1(s):
        return unit(jnp.minimum(s, n_units - 1))

    def stage3(s):
        return unit(jnp.maximum(s - 2, 0))

    def q_map(s):
        b, p, i = stage1(s)
        return b, p, q_blk0 + i, 0

    def k_map(s):
        b, p, _ = stage1(s)
        return b, p, k_blk0, 0

    def v_map(s):
        b, p, _ = stage3(s)
        return b, p, 0, k_blk0

    def o_map(s):
        b, p, i = stage3(s)
        return b, i, p

    in_specs = [
        pl.BlockSpec((1, qh, tq, HEAD_PAD), q_map),
        pl.BlockSpec((1, n_kv, t_blk, HEAD_PAD), k_map),
        pl.BlockSpec((1, n_kv, VT_ROWS, t_blk), v_map),
    ]
    args = [q, k, vt]
    if diff_args is not None:
        lam_spec = pl.BlockSpec((1, 2, DIFF_QK), lambda s: (stage3(s)[1], 0, 0))
        in_specs += [lam_spec] * 4 + [pl.BlockSpec((V_DIM, 1), lambda s: (0, 0))]
        args += list(diff_args)
    score_buf = pltpu.VMEM((n_dots, t_blk, SCORE_COLS), F32)
    prob_buf = pltpu.VMEM((n_dots, t_blk, SCORE_COLS), BF16)
    max_buf = pltpu.VMEM((n_dots, 1, SCORE_COLS), F32)
    return pl.pallas_call(
        functools.partial(_attn_kernel, n_kv=n_kv, group=group, diff=diff_args is not None, lam_init=lam_init),
        grid=(n_units + 2,),
        in_specs=in_specs,
        out_specs=pl.BlockSpec((1, tq, LANES), o_map),
        out_shape=jax.ShapeDtypeStruct((B, n_q * tq, n_pairs * LANES), BF16),
        scratch_shapes=[score_buf, max_buf, max_buf, prob_buf, prob_buf],
        compiler_params=_cparams(("arbitrary",)),
        name=name,
    )(*args)


def _na_kernel(q_ref, k_ref, v_ref, bias_ref, o_ref, *, n_lat, win):
    g = pl.program_id(1)
    rows = n_lat // GRID_W
    start = jnp.clip(NA_GROUP_ROWS * g - NA_ROWS // 2, 0, rows - NA_WIN_ROWS) * GRID_W
    start = pl.multiple_of(start, GRID_W)
    tq = q_ref.shape[2]
    lane = lax.broadcasted_iota(jnp.int32, (tq, LANES), 1)
    outs = []
    for h in range(NA_HEADS):
        q = q_ref[0, h]
        s_l = lax.dot_general(q, k_ref[0, h, pl.ds(start, win), :], NT_DIMS, preferred_element_type=F32)
        s_l = s_l + bias_ref[0, h]
        s_c = lax.dot_general(q, k_ref[0, h, n_lat:, :], NT_DIMS, preferred_element_type=F32)
        m = jnp.maximum(jnp.max(s_l, axis=-1, keepdims=True), jnp.max(s_c, axis=-1, keepdims=True))
        p_l = jnp.exp2(s_l - m).astype(BF16)
        p_c = jnp.exp2(s_c - m).astype(BF16)
        oe = _bdot(p_l, v_ref[0, h, pl.ds(start, win), :]) + _bdot(p_c, v_ref[0, h, n_lat:, :])
        outs.append(oe[:, :LANES] / oe[:, LANES:])
    out = jnp.concatenate([jnp.where(lane < 64, outs[0], outs[1]), jnp.where(lane < 64, outs[2], outs[3])], axis=-1)
    o_ref[0] = out.astype(BF16)


def _na_call(q, k, v, bias, n_lat, layer):
    B, H, T, _ = q.shape
    tq = NA_GROUP_ROWS * GRID_W
    win = NA_WIN_ROWS * GRID_W
    n_groups = n_lat // tq

    def variant(g):
        return 3 * layer + jnp.where(g == 0, 0, jnp.where(g == n_groups - 1, 2, 1))

    return pl.pallas_call(
        functools.partial(_na_kernel, n_lat=n_lat, win=win),
        grid=(B, n_groups),
        in_specs=[
            pl.BlockSpec((1, H, tq, HEAD_PAD), lambda b, g: (b, 0, g, 0)),
            pl.BlockSpec((1, H, T, HEAD_PAD), lambda b, g: (b, 0, 0, 0)),
            pl.BlockSpec((1, H, T, V_PAD), lambda b, g: (b, 0, 0, 0)),
            pl.BlockSpec((1, H, tq, win), lambda b, g: (variant(g), 0, 0, 0)),
        ],
        out_specs=pl.BlockSpec((1, tq, H * 64), lambda b, g: (b, g, 0)),
        out_shape=jax.ShapeDtypeStruct((B, n_lat, H * 64), BF16),
        compiler_params=_cparams(("parallel", "arbitrary")),
        name="na_attn",
    )(q, k, v, bias)


def _merge_kernel(x_ref, mod_ref, ya_ref, yb_ref, yc_ref, yd_ref, wg_ref, wb_ref, wo_ref, lg_ref, lb_ref, o_ref):
    D = D_MODEL
    x = x_ref[0]
    mod = mod_ref[0]
    sh, sc, gate_res = mod[:, 0:D], mod[:, D:2 * D], mod[:, 2 * D:3 * D]
    h = (x * (1.0 + sc) + sh).astype(BF16)
    acc = None
    for i, y_ref in enumerate((ya_ref, yb_ref, yc_ref, yd_ref)):
        gate = _sigmoid(_bdot(h, wg_ref[0, :, i * D:(i + 1) * D]))
        term = gate * _bdot(y_ref[0], wb_ref[0, i])
        acc = term if acc is None else acc + term
    out = _bdot(acc.astype(BF16), wo_ref[0])
    o_ref[0] = _layer_norm(DEEPNORM_ALPHA * x + gate_res * out, lg_ref[...], lb_ref[...])


def _merge_call(x, mod, ys, wg, wb, wo, lg, lb, tm, layer):
    B, S, D = x.shape
    per_batch_mod = mod.shape[0] > 1
    tok = lambda w: pl.BlockSpec((1, tm, w), lambda b, i: (b, i, 0))
    return pl.pallas_call(
        _merge_kernel,
        grid=(B, S // tm),
        in_specs=[tok(D), pl.BlockSpec((1, 1, 6 * D), lambda b, i: (b if per_batch_mod else 0, 0, 0))]
        + [tok(BRANCH_W)] * 4
        + [_layer_spec(wg, layer), _layer_spec(wb, layer), _layer_spec(wo, layer), _const_spec(lg.shape), _const_spec(lb.shape)],
        out_specs=tok(D),
        out_shape=jax.ShapeDtypeStruct((B, S, D), F32),
        compiler_params=_cparams(("parallel", "arbitrary")),
        name="merge_ln",
    )(x, mod, *ys, wg, wb, wo, lg, lb)


def _ffn_kernel(x_ref, mod_ref, wgu_ref, wd_ref, lg_ref, lb_ref, o_ref):
    D = D_MODEL
    x = x_ref[0]
    mod = mod_ref[0]
    sh, sc, gate_res = mod[:, 3 * D:4 * D], mod[:, 4 * D:5 * D], mod[:, 5 * D:6 * D]
    h = (x * (1.0 + sc) + sh).astype(BF16)
    gu = _bdot(h, wgu_ref[0])
    g, u = gu[:, :D_FF], gu[:, D_FF:]
    act = (g * _sigmoid(g) * u).astype(BF16)
    down = _bdot(act, wd_ref[0])
    o_ref[0] = _layer_norm(DEEPNORM_ALPHA * x + gate_res * down, lg_ref[...], lb_ref[...])


def _ffn_call(x, mod, wgu, wd, lg, lb, tm, layer):
    B, S, D = x.shape
    per_batch_mod = mod.shape[0] > 1
    tok = pl.BlockSpec((1, tm, D), lambda b, i: (b, i, 0))
    return pl.pallas_call(
        _ffn_kernel,
        grid=(B, S // tm),
        in_specs=[tok, pl.BlockSpec((1, 1, 6 * D), lambda b, i: (b if per_batch_mod else 0, 0, 0)),
                  _layer_spec(wgu, layer), _layer_spec(wd, layer), _const_spec(lg.shape), _const_spec(lb.shape)],
        out_specs=tok,
        out_shape=jax.ShapeDtypeStruct((B, S, D), F32),
        compiler_params=_cparams(("parallel", "arbitrary")),
        name="ffn_ln",
    )(x, mod, wgu, wd, lg, lb)


def kernel(x, c, ctx, c_ctx, w_ada, b_ada, w_in, g_q_a, w_q_up, g_kv_a, w_kv_up, rpb, lam_q1, lam_k1, lam_q2, lam_k2,
           g_sub, g_qn, g_kn, w_branch, w_out, ln1_g, ln1_b, w_gate_up, w_down, ln2_g, ln2_b):
    B, S, D = x.shape
    rows = S // GRID_W
    n_q = S // Q_TILE
    T = S + CTX_LEN
    ctx_blk = S // CTX_LEN

    c8 = jnp.concatenate([c, c_ctx[None, :], jnp.zeros((8 - B - 1, D), F32)], axis=0)
    mod = _ada_call(c8, w_ada, b_ada)
    tabs = _rope_tables(S)
    wsm = _small_weights(w_in)
    wg = w_in[..., GATE_OFF:].astype(BF16)
    wqup = _pad_heads(w_q_up, MLA_HEADS, MLA_NOPE + MLA_ROPE).astype(BF16)
    wkvk, wkvv = _kv_up_weights(w_kv_up)
    wb, wo = w_branch.astype(BF16), w_out.astype(BF16)
    wgu, wd = w_gate_up.astype(BF16), w_down.astype(BF16)
    bias = _na_bias_tables(rpb, rows)
    row = lambda v: v.reshape(1, -1)
    pad64 = lambda v: jnp.concatenate([v, jnp.zeros_like(v)]).reshape(1, LANES)

    xl, xc = x, ctx
    for l in range(DEPTH):
        need_ctx = l < DEPTH - 1
        lam_init = 0.8 - 0.6 * math.exp(-0.3 * l)
        mod_l = mod[l, :B].reshape(B, 1, 6 * D)
        mod_c = mod[l, B:B + 1].reshape(1, 1, 6 * D)
        (qa, ka, vat, qn, kn, vn, qd, kd, vdt, qg, kg, vgt) = _proj_call(
            xl, xc, mod_l, mod_c, wsm, row(g_q_a[l]), wqup, row(g_kv_a[l]), wkvk, wkvv,
            pad64(g_qn[l]), pad64(g_kn[l]), tabs, layer=l)

        diff_args = (lam_q1[l].reshape(2, 2, DIFF_QK), lam_k1[l].reshape(2, 2, DIFF_QK),
                     lam_q2[l].reshape(2, 2, DIFF_QK), lam_k2[l].reshape(2, 2, DIFF_QK),
                     g_sub[l].reshape(V_DIM, 1))

        lat = lambda tq: dict(tq=tq, q_blk0=0, n_q=S // tq, t_blk=T, k_blk0=0)
        ya = _attn_call(qa, ka, vat, group=1, n_kv=2, name="mla_attn", **lat(ATTN_TILE))
        yb = _na_call(qn, kn, vn, bias, S, layer=l)
        yc = _attn_call(qd, kd, vdt, group=2, n_kv=2, diff_args=diff_args, lam_init=lam_init, name="diff_attn",
                        **lat(ATTN_TILE // 2))
        yd = _attn_call(qg, kg, vgt, group=2, n_kv=1, name="gqa_attn", **lat(ATTN_TILE))

        ln1 = (row(ln1_g[l]), row(ln1_b[l]))
        ln2 = (row(ln2_g[l]), row(ln2_b[l]))

        if need_ctx:
            cx = dict(tq=CTX_LEN, q_blk0=ctx_blk, n_q=1, t_blk=CTX_LEN, k_blk0=ctx_blk)
            ya_c = _attn_call(qa, ka, vat, group=1, n_kv=2, name="mla_attn_ctx", **cx)
            ctx0 = dict(tq=CTX_LEN, q_blk0=0, n_q=1, t_blk=CTX_LEN, k_blk0=0)
            yb_c = _attn_call(qn[:, :, S:], kn[:, :, S:], _transpose_v(vn[:, :, S:]), group=1, n_kv=2,
                              name="na_attn_ctx", **ctx0)
            yc_c = _attn_call(qd, kd, vdt, group=2, n_kv=2, diff_args=diff_args, lam_init=lam_init,
                              name="diff_attn_ctx", **cx)
            yd_c = _attn_call(qg, kg, vgt, group=2, n_kv=1, name="gqa_attn_ctx", **cx)
            xc1 = _merge_call(xc, mod_c, (ya_c, yb_c, yc_c, yd_c), wg, wb, wo, *ln1, tm=CTX_LEN, layer=l)
            xc = _ffn_call(xc1, mod_c, wgu, wd, *ln2, tm=CTX_LEN, layer=l)

        xl1 = _merge_call(xl, mod_l, (ya, yb, yc, yd), wg, wb, wo, *ln1, tm=Q_TILE, layer=l)
        xl = _ffn_call(xl1, mod_l, wgu, wd, *ln2, tm=Q_TILE, layer=l)
    return xl
```

```python
import functools
import math

import jax
import jax.numpy as jnp
import numpy as np
from jax import lax
from jax.experimental import pallas as pl
from jax.experimental.pallas import tpu as pltpu

F32 = jnp.float32
BF16 = jnp.bfloat16

D_MODEL = 1024
DEPTH = 2
CTX_LEN = 256
GRID_W = 64
ROPE_THETA = 10000.0
EPS = 1e-6

MLA_HEADS = 4
MLA_Q_RANK = 384
MLA_KV_RANK = 256
MLA_NOPE = 64
MLA_ROPE = 32
MLA_V = 64
NA_HEADS = 4
NA_DIM = 64
NA_ROWS = 8
NA_COLS = 16
DIFF_HEADS = 4
DIFF_QK = 32
DIFF_V = 64
GQA_HEADS = 4
GQA_KV_HEADS = 2
GQA_DIM = 64
N_BRANCH = 4
BRANCH_W = 256
D_FF = -(-8 * D_MODEL // (3 * 256)) * 256
DEEPNORM_ALPHA = (2 * DEPTH) ** 0.25

IN_SIZES = (
    MLA_Q_RANK, MLA_KV_RANK, MLA_ROPE,
    NA_HEADS * NA_DIM, NA_HEADS * NA_DIM, NA_HEADS * NA_DIM,
    DIFF_HEADS * 2 * DIFF_QK, DIFF_HEADS * 2 * DIFF_QK, DIFF_HEADS * DIFF_V,
    GQA_HEADS * GQA_DIM, GQA_KV_HEADS * GQA_DIM, GQA_KV_HEADS * GQA_DIM,
    N_BRANCH * D_MODEL,
)
IN_OFFS = tuple(int(v) for v in np.concatenate([[0], np.cumsum(IN_SIZES)]))
GATE_OFF = IN_OFFS[12]

LANES = 128
HEAD_PAD = LANES
V_PAD = 2 * LANES
TOK_TILE = 256
Q_TILE = 512
ATTN_TILE = 512
NA_GROUP_ROWS = 4
NA_WIN_ROWS = 12
SCORE_COLS = 256
MAX_CHAINS = 8
KEY_TILE = 256
KEY_CHUNKS = 4
V_DIM = 64
VT_ROWS = 80
MASK_VALUE = -1e30
VMEM_LIMIT = 56 * 1024 * 1024

LOG2E = math.log2(math.e)
MLA_SCALE = (MLA_NOPE + MLA_ROPE) ** -0.5 * LOG2E
NA_SCALE = NA_DIM ** -0.5 * LOG2E
DIFF_SCALE = DIFF_QK ** -0.5 * LOG2E
GQA_SCALE = GQA_DIM ** -0.5 * LOG2E

P_CQ = 0
P_CKV = P_CQ + MLA_Q_RANK
P_KPE = P_CKV + MLA_KV_RANK
P_QN = P_KPE + HEAD_PAD
P_KN = P_QN + NA_HEADS * HEAD_PAD
P_VN = P_KN + NA_HEADS * HEAD_PAD
P_QD = P_VN + NA_HEADS * HEAD_PAD
P_KD = P_QD + DIFF_HEADS * HEAD_PAD
P_VD = P_KD + DIFF_HEADS * HEAD_PAD
P_QG = P_VD + DIFF_HEADS * HEAD_PAD
P_KG = P_QG + GQA_HEADS * HEAD_PAD
P_VG = P_KG + GQA_KV_HEADS * HEAD_PAD
P_END = P_VG + GQA_KV_HEADS * HEAD_PAD

NT_DIMS = (((1,), (1,)), ((), ()))


def _cparams(sem):
    return pltpu.CompilerParams(dimension_semantics=sem, vmem_limit_bytes=VMEM_LIMIT)


def _const_spec(shape):
    nd = len(shape)
    return pl.BlockSpec(shape, lambda *_: (0,) * nd, pipeline_mode=pl.Buffered(1))


def _layer_spec(w, layer):
    nd = w.ndim - 1
    return pl.BlockSpec((1,) + w.shape[1:], lambda *_: (layer,) + (0,) * nd, pipeline_mode=pl.Buffered(1))


def _pad_heads(w, n_heads, width, dup=False):
    lead = w.shape[:-1]
    h = w.reshape(lead + (n_heads, width))
    fill = h if dup else jnp.zeros(lead + (n_heads, HEAD_PAD - width), w.dtype)
    return jnp.concatenate([h, fill], axis=-1).reshape(lead + (n_heads * HEAD_PAD,))


def _small_weights(w_in):
    o = IN_OFFS
    seg = lambda i: w_in[..., o[i]:o[i + 1]]
    zeros = lambda n: jnp.zeros(w_in.shape[:-1] + (n,), w_in.dtype)
    parts = [seg(0), seg(1), zeros(MLA_NOPE), seg(2), zeros(HEAD_PAD - MLA_NOPE - MLA_ROPE),
             _pad_heads(seg(3), NA_HEADS, 64), _pad_heads(seg(4), NA_HEADS, 64), _pad_heads(seg(5), NA_HEADS, 64, True),
             _pad_heads(seg(6), DIFF_HEADS, 64), _pad_heads(seg(7), DIFF_HEADS, 64), _pad_heads(seg(8), DIFF_HEADS, 64, True),
             _pad_heads(seg(9), GQA_HEADS, 64), _pad_heads(seg(10), GQA_KV_HEADS, 64),
             _pad_heads(seg(11), GQA_KV_HEADS, 64, True)]
    out = jnp.concatenate(parts, axis=-1)
    assert out.shape[-1] == P_END
    return out.astype(BF16)


def _kv_up_weights(w_kv_up):
    lead = w_kv_up.shape[:-1]
    h = w_kv_up.reshape(lead + (MLA_HEADS, MLA_NOPE + MLA_V))
    k = _pad_heads(h[..., :MLA_NOPE].reshape(lead + (-1,)), MLA_HEADS, MLA_NOPE)
    v = _pad_heads(h[..., MLA_NOPE:].reshape(lead + (-1,)), MLA_HEADS, MLA_V, True)
    return k.astype(BF16), v.astype(BF16)


def _rope_pattern(S, rot_dim):
    t = jnp.arange(S)
    pos = jnp.stack([t // GRID_W, t % GRID_W], axis=-1).astype(F32)
    n_f = rot_dim // 4
    inv = ROPE_THETA ** (-jnp.arange(n_f, dtype=F32) / n_f)
    ang = pos[:, :, None] * inv
    cos, sin = jnp.cos(ang), jnp.sin(ang)
    c = jnp.concatenate([cos[:, 0], cos[:, 0], cos[:, 1], cos[:, 1]], axis=-1)
    s = jnp.concatenate([-sin[:, 0], sin[:, 0], -sin[:, 1], sin[:, 1]], axis=-1)
    c = jnp.concatenate([c, jnp.ones((CTX_LEN, rot_dim), F32)], axis=0)
    s = jnp.concatenate([s, jnp.zeros((CTX_LEN, rot_dim), F32)], axis=0)
    return c, s


def _rope_tables(S):
    T = S + CTX_LEN
    c32, s32 = _rope_pattern(S, 32)
    c64, s64 = _rope_pattern(S, 64)
    one = lambda n: jnp.ones((T, n), F32)
    zero = lambda n: jnp.zeros((T, n), F32)
    cat = lambda xs: jnp.concatenate(xs, axis=-1)
    ca, sa = cat([one(64), c32, one(32)]), cat([zero(64), s32, zero(32)])
    cd, sd = cat([c32, c32, one(64)]), cat([s32, s32, zero(64)])
    cg, sg = cat([c64, one(64)]), cat([s64, zero(64)])
    return ca, sa, cd, sd, cg, sg


def _na_bias_tables(rpb, rows):
    depth, n_heads = rpb.shape[:2]
    n_groups = rows // NA_GROUP_ROWS
    kh, kw = min(NA_ROWS, rows), NA_COLS
    w = np.arange(GRID_W)[:, None]
    c = np.arange(GRID_W)[None, :]
    cs = np.clip(w - kw // 2, 0, GRID_W - kw)
    col_valid = (c >= cs) & (c < cs + kw)
    col_off = c - w + (NA_COLS - 1)
    onehot_c = ((col_off[None] == np.arange(2 * NA_COLS - 1)[:, None, None]) & col_valid[None]).astype(np.float32)
    a = np.arange(NA_GROUP_ROWS)[:, None]
    i = np.arange(NA_WIN_ROWS)[None, :]
    out = []
    for g in (0, 1, n_groups - 1):
        r = NA_GROUP_ROWS * g + a
        i_abs = np.clip(NA_GROUP_ROWS * g - kh // 2, 0, rows - NA_WIN_ROWS) + i
        rs = np.clip(r - kh // 2, 0, rows - kh)
        row_valid = (i_abs >= rs) & (i_abs < rs + kh)
        row_off = i_abs - r + (NA_ROWS - 1)
        onehot_r = ((row_off[..., None] == np.arange(2 * NA_ROWS - 1)) & row_valid[..., None]).astype(np.float32)
        valid = row_valid[:, None, :, None] & col_valid[None, :, None, :]
        b = jnp.einsum('air,lhrk,kwc->lhawic', jnp.asarray(onehot_r), rpb.astype(F32), jnp.asarray(onehot_c),
                       precision=lax.Precision.HIGHEST)
        b = jnp.where(jnp.asarray(valid)[None, None], b * LOG2E, MASK_VALUE)
        out.append(b.reshape(depth, n_heads, NA_GROUP_ROWS * GRID_W, NA_WIN_ROWS * GRID_W))
    return jnp.stack(out, axis=1).reshape((depth * 3,) + out[0].shape[1:])


def _sigmoid(x):
    return 1.0 / (1.0 + jnp.exp(-x))


def _rms(x, g, n):
    ms = jnp.sum(x * x, axis=-1, keepdims=True) * (1.0 / n)
    return x * lax.rsqrt(ms + EPS) * g


def _layer_norm(x, g, b):
    mu = jnp.mean(x, axis=-1, keepdims=True)
    xc = x - mu
    var = jnp.mean(xc * xc, axis=-1, keepdims=True)
    return xc * lax.rsqrt(var + EPS) * g + b


def _rope(z, c, s, n_f):
    lane = lax.broadcasted_iota(jnp.int32, z.shape, 1)
    first = (lane & n_f) == 0
    up = pltpu.roll(z, LANES - n_f, 1)
    dn = pltpu.roll(z, n_f, 1)
    return z * c + jnp.where(first, up, dn) * s


def _bdot(a, b):
    return jnp.dot(a, b, preferred_element_type=F32)


def _ada_kernel(c_ref, w_ref, b_ref, o_ref):
    c = c_ref[...]
    a = c * _sigmoid(c)
    o_ref[0] = jnp.dot(a, w_ref[0], preferred_element_type=F32, precision=lax.Precision.HIGHEST) + b_ref[0]


def _ada_call(c8, w_ada, b_ada):
    L, D, N = w_ada.shape
    tn = 1536
    return pl.pallas_call(
        _ada_kernel,
        grid=(L, N // tn),
        in_specs=[
            pl.BlockSpec((8, D), lambda l, n: (0, 0)),
            pl.BlockSpec((1, D, tn), lambda l, n: (l, 0, n)),
            pl.BlockSpec((1, 1, tn), lambda l, n: (l, 0, n)),
        ],
        out_specs=pl.BlockSpec((1, 8, tn), lambda l, n: (l, 0, n)),
        out_shape=jax.ShapeDtypeStruct((L, 8, N), F32),
        compiler_params=_cparams(("parallel", "parallel")),
        name="ada_mod",
    )(c8, w_ada, b_ada.reshape(L, 1, N))


def _proj_kernel(x_ref, ctx_ref, modl_ref, modc_ref, w_ref, gqa_ref, wqup_ref, gkva_ref, wkvk_ref, wkvv_ref,
                 gqn_ref, gkn_ref, ca_ref, sa_ref, cd_ref, sd_ref, cg_ref, sg_ref,
                 qa_ref, ka_ref, va_ref, qn_ref, kn_ref, vn_ref, qd_ref, kd_ref, vd_ref, qg_ref, kg_ref, vg_ref,
                 *, n_lat_tiles):
    D = D_MODEL
    is_ctx = pl.program_id(0) == n_lat_tiles
    x = jnp.where(is_ctx, ctx_ref[0], x_ref[0])
    ml, mc = modl_ref[0], modc_ref[0]
    sh = jnp.where(is_ctx, mc[:, 0:D], ml[:, 0:D])
    sc = jnp.where(is_ctx, mc[:, D:2 * D], ml[:, D:2 * D])
    h = (x * (1.0 + sc) + sh).astype(BF16)
    z = _bdot(h, w_ref[0])
    tm = z.shape[0]
    ones = jnp.ones((tm, LANES), F32)
    lane = lax.broadcasted_iota(jnp.int32, (tm, LANES), 1)
    head = lambda base, i: z[:, base + HEAD_PAD * i: base + HEAD_PAD * (i + 1)]
    with_ones = lambda v: jnp.concatenate([v, ones], axis=-1).astype(BF16)
    tail = (lax.broadcasted_iota(jnp.int32, (VT_ROWS - V_DIM, tm), 0) == 0).astype(F32)
    transposed = lambda v: jnp.concatenate([v.T[:V_DIM], tail], axis=0).astype(BF16)

    ca, sa = ca_ref[...], sa_ref[...]
    cqn = _rms(z[:, P_CQ:P_CKV], gqa_ref[...], MLA_Q_RANK).astype(BF16)
    qa_full = _bdot(cqn, wqup_ref[0])
    ckvn = _rms(z[:, P_CKV:P_KPE], gkva_ref[...], MLA_KV_RANK).astype(BF16)
    k_full = _bdot(ckvn, wkvk_ref[0])
    v_full = _bdot(ckvn, wkvv_ref[0])
    kpe = _rope(z[:, P_KPE:P_QN], ca, sa, MLA_ROPE // 4)
    for i in range(MLA_HEADS):
        sl = slice(HEAD_PAD * i, HEAD_PAD * (i + 1))
        qa_ref[0, i] = (_rope(qa_full[:, sl], ca, sa, MLA_ROPE // 4) * MLA_SCALE).astype(BF16)
        ka_ref[0, i] = (k_full[:, sl] + kpe).astype(BF16)
        va_ref[0, i] = transposed(v_full[:, sl])

    for i in range(NA_HEADS):
        qn_ref[0, i] = (head(P_QN, i) * NA_SCALE).astype(BF16)
        kn_ref[0, i] = head(P_KN, i).astype(BF16)
        vn_ref[0, i] = with_ones(head(P_VN, i))

    cd, sd = cd_ref[...], sd_ref[...]
    for i in range(DIFF_HEADS):
        zq = _rope(head(P_QD, i), cd, sd, DIFF_QK // 4) * DIFF_SCALE
        qd_ref[0, 2 * i] = jnp.where(lane < DIFF_QK, zq, 0.0).astype(BF16)
        qd_ref[0, 2 * i + 1] = jnp.where(lane >= DIFF_QK, zq, 0.0).astype(BF16)
        kd_ref[0, i] = _rope(head(P_KD, i), cd, sd, DIFF_QK // 4).astype(BF16)
        vd_ref[0, i] = transposed(head(P_VD, i))

    cg, sg = cg_ref[...], sg_ref[...]
    gqn, gkn = gqn_ref[...], gkn_ref[...]
    for i in range(GQA_HEADS):
        qn = _rms(head(P_QG, i), gqn, GQA_DIM)
        qg_ref[0, i] = (_rope(qn, cg, sg, GQA_DIM // 4) * GQA_SCALE).astype(BF16)
    for i in range(GQA_KV_HEADS):
        kn = _rms(head(P_KG, i), gkn, GQA_DIM)
        kg_ref[0, i] = _rope(kn, cg, sg, GQA_DIM // 4).astype(BF16)
        vg_ref[0, i] = transposed(head(P_VG, i))


def _proj_call(xl, xc, mod_l, mod_c, wsm, gqa, wqup, gkva, wkvk, wkvv, gqn, gkn, tabs, layer):
    B, S, D = xl.shape
    tm = TOK_TILE
    n_lat = S // tm
    T = S + CTX_LEN
    grid = (n_lat + 1, B)
    tab_spec = pl.BlockSpec((tm, LANES), lambda t, b: (t, 0))
    in_specs = [
        pl.BlockSpec((1, tm, D), lambda t, b: (b, jnp.minimum(t, n_lat - 1), 0)),
        pl.BlockSpec((1, tm, D), lambda t, b: (b, 0, 0)),
        pl.BlockSpec((1, 1, 6 * D), lambda t, b: (b, 0, 0)),
        pl.BlockSpec((1, 1, 6 * D), lambda t, b: (0, 0, 0)),
        _layer_spec(wsm, layer), _const_spec(gqa.shape), _layer_spec(wqup, layer), _const_spec(gkva.shape),
        _layer_spec(wkvk, layer), _layer_spec(wkvv, layer), _const_spec(gqn.shape), _const_spec(gkn.shape),
    ] + [tab_spec] * 6

    def o(n_heads, width):
        return (jax.ShapeDtypeStruct((B, n_heads, T, width), BF16),
                pl.BlockSpec((1, n_heads, tm, width), lambda t, b: (b, 0, t, 0)))

    def ot(n_heads):
        return (jax.ShapeDtypeStruct((B, n_heads, VT_ROWS, T), BF16),
                pl.BlockSpec((1, n_heads, VT_ROWS, tm), lambda t, b: (b, 0, 0, t)))

    outs = [o(MLA_HEADS, HEAD_PAD), o(MLA_HEADS, HEAD_PAD), ot(MLA_HEADS),
            o(NA_HEADS, HEAD_PAD), o(NA_HEADS, HEAD_PAD), o(NA_HEADS, V_PAD),
            o(2 * DIFF_HEADS, HEAD_PAD), o(DIFF_HEADS, HEAD_PAD), ot(DIFF_HEADS),
            o(GQA_HEADS, HEAD_PAD), o(GQA_KV_HEADS, HEAD_PAD), ot(GQA_KV_HEADS)]
    return pl.pallas_call(
        functools.partial(_proj_kernel, n_lat_tiles=n_lat),
        grid=grid,
        in_specs=in_specs,
        out_specs=[s for _, s in outs],
        out_shape=[s for s, _ in outs],
        compiler_params=_cparams(("arbitrary", "arbitrary")),
        name="in_proj",
    )(xl, xc, mod_l, mod_c, wsm, gqa, wqup, gkva, wkvk, wkvv, gqn, gkn, *tabs)


def _key_chunks(t_keys):
    n_tiles = t_keys // KEY_TILE
    n_chunks = min(KEY_CHUNKS, n_tiles)
    sizes = [n_tiles // n_chunks + (1 if c < n_tiles % n_chunks else 0) for c in range(n_chunks)]
    bounds = np.cumsum([0] + sizes) * KEY_TILE
    return [(int(bounds[c]), int(bounds[c + 1])) for c in range(n_chunks)]


def _attn_kernel(*refs, n_kv, group, diff, lam_init):
    q_ref, k_ref, v_ref = refs[:3]
    s_ref, ma_ref, mb_ref, pa_ref, pb_ref = refs[-5:]
    step = pl.program_id(0)
    tq = q_ref.shape[2]
    n_dots = s_ref.shape[0]
    cols = s_ref.shape[2]
    split = n_dots // n_kv

    @pl.when(step == 0)
    def _():
        s_ref[...] = jnp.zeros(s_ref.shape, F32)
        mb_ref[...] = jnp.zeros(mb_ref.shape, F32)
        pb_ref[...] = jnp.ones(pb_ref.shape, BF16)

    def body(m_write, m_read, p_write, p_read):
        qs = []
        for j in range(n_kv):
            q = q_ref[0, j * group:(j + 1) * group].reshape(group * tq, HEAD_PAD)
            qs += [q[r * cols:(r + 1) * cols] for r in range(split)]
        acc = [None] * n_dots
        run_max = [None] * n_dots
        for lo, hi in _key_chunks(k_ref.shape[2]):
            for d in range(n_dots):
                j = d // split
                part = _bdot(v_ref[0, j, :, lo:hi], p_read[d, lo:hi, :])
                acc[d] = part if acc[d] is None else acc[d] + part
                p_write[d, lo:hi, :] = jnp.exp2(s_ref[d, lo:hi, :] - m_read[d]).astype(BF16)
                s = lax.dot_general(k_ref[0, j, lo:hi, :], qs[d], NT_DIMS, preferred_element_type=F32)
                s_ref[d, lo:hi, :] = s
                pm = jnp.max(s.reshape(MAX_CHAINS, (hi - lo) // MAX_CHAINS, cols), axis=1)
                run_max[d] = pm if run_max[d] is None else jnp.maximum(run_max[d], pm)
        heads = []
        for j in range(n_kv):
            outs = []
            for d in range(j * split, (j + 1) * split):
                m_write[d] = jnp.max(run_max[d], axis=0, keepdims=True)
                outs.append(acc[d][:V_DIM] / acc[d][V_DIM:V_DIM + 1])
            on = outs[0] if split == 1 else jnp.concatenate(outs, axis=1)
            heads += [on[:, g * tq:(g + 1) * tq] for g in range(group)]
        _attn_finish(heads, refs, diff, lam_init)

    parity = lax.rem(step, 2)

    @pl.when(parity == 0)
    def _():
        body(ma_ref, mb_ref, pa_ref, pb_ref)

    @pl.when(parity == 1)
    def _():
        body(mb_ref, ma_ref, pb_ref, pa_ref)


def _attn_finish(heads, refs, diff, lam_init):
    if diff:
        lq1_ref, lk1_ref, lq2_ref, lk2_ref, gsub_ref, o_ref = refs[3:9]
    else:
        o_ref = refs[3]
    if diff:
        ys = []
        for j in range(2):
            dot_l = lambda a, b: jnp.sum(a[0, j:j + 1] * b[0, j:j + 1], axis=-1, keepdims=True)
            lam = jnp.exp(dot_l(lq1_ref, lk1_ref)) - jnp.exp(dot_l(lq2_ref, lk2_ref)) + lam_init
            y = heads[2 * j] - lam * heads[2 * j + 1]
            ms = jnp.mean(y * y, axis=0, keepdims=True)
            ys.append(y * lax.rsqrt(ms + EPS) * gsub_ref[...] * (1.0 - lam_init))
        heads = ys
    o_ref[0] = jnp.concatenate(heads, axis=0).T.astype(BF16)


def _transpose_v(v):
    B, H, T, _ = v.shape
    vt = jnp.swapaxes(v[..., :V_DIM], -1, -2)
    tail = jnp.zeros((B, H, VT_ROWS - V_DIM, T), v.dtype).at[:, :, 0, :].set(1.0)
    return jnp.concatenate([vt, tail], axis=2)


def _attn_call(q, k, vt, *, group, n_kv, tq, q_blk0, n_q, t_blk, k_blk0, diff_args=None, lam_init=0.0, name):
    B, Hq = q.shape[0], q.shape[1]
    n_pairs = Hq // (n_kv * group)
    qh = n_kv * group
    n_units = B * n_pairs * n_q
    n_dots = qh * tq // SCORE_COLS

    def unit(u):
        return u // (n_pairs * n_q), (u // n_q) % n_pairs, u % n_q

    def stage1(s):
        return unit(jnp.minimum(s, n_units - 1))

    def stage3(s):
        return unit(jnp.maximum(s - 2, 0))

    def q_map(s):
        b, p, i = stage1(s)
        return b, p, q_blk0 + i, 0

    def k_map(s):
        b, p, _ = stage1(s)
        return b, p, k_blk0, 0

    def v_map(s):
        b, p, _ = stage3(s)
        return b, p, 0, k_blk0

    def o_map(s):
        b, p, i = stage3(s)
        return b, i, p

    in_specs = [
        pl.BlockSpec((1, qh, tq, HEAD_PAD), q_map),
        pl.BlockSpec((1, n_kv, t_blk, HEAD_PAD), k_map),
        pl.BlockSpec((1, n_kv, VT_ROWS, t_blk), v_map),
    ]
    args = [q, k, vt]
    if diff_args is not None:
        lam_spec = pl.BlockSpec((1, 2, DIFF_QK), lambda s: (stage3(s)[1], 0, 0))
        in_specs += [lam_spec] * 4 + [pl.BlockSpec((V_DIM, 1), lambda s: (0, 0))]
        args += list(diff_args)
    score_buf = pltpu.VMEM((n_dots, t_blk, SCORE_COLS), F32)
    prob_buf = pltpu.VMEM((n_dots, t_blk, SCORE_COLS), BF16)
    max_buf = pltpu.VMEM((n_dots, 1, SCORE_COLS), F32)
    return pl.pallas_call(
        functools.partial(_attn_kernel, n_kv=n_kv, group=group, diff=diff_args is not None, lam_init=lam_init),
        grid=(n_units + 2,),
        in_specs=in_specs,
        out_specs=pl.BlockSpec((1, tq, LANES), o_map),
        out_shape=jax.ShapeDtypeStruct((B, n_q * tq, n_pairs * LANES), BF16),
        scratch_shapes=[score_buf, max_buf, max_buf, prob_buf, prob_buf],
        compiler_params=_cparams(("arbitrary",)),
        name=name,
    )(*args)


def _na_kernel(q_ref, k_ref, v_ref, bias_ref, o_ref, *, n_lat, win):
    g = pl.program_id(1)
    rows = n_lat // GRID_W
    start = jnp.clip(NA_GROUP_ROWS * g - NA_ROWS // 2, 0, rows - NA_WIN_ROWS) * GRID_W
    start = pl.multiple_of(start, GRID_W)
    tq = q_ref.shape[2]
    lane = lax.broadcasted_iota(jnp.int32, (tq, LANES), 1)
    outs = []
    for h in range(NA_HEADS):
        q = q_ref[0, h]
        s_l = lax.dot_general(q, k_ref[0, h, pl.ds(start, win), :], NT_DIMS, preferred_element_type=F32)
        s_l = s_l + bias_ref[0, h]
        s_c = lax.dot_general(q, k_ref[0, h, n_lat:, :], NT_DIMS, preferred_element_type=F32)
        m = jnp.maximum(jnp.max(s_l, axis=-1, keepdims=True), jnp.max(s_c, axis=-1, keepdims=True))
        p_l = jnp.exp2(s_l - m).astype(BF16)
        p_c = jnp.exp2(s_c - m).astype(BF16)
        oe = _bdot(p_l, v_ref[0, h, pl.ds(start, win), :]) + _bdot(p_c, v_ref[0, h, n_lat:, :])
        outs.append(oe[:, :LANES] / oe[:, LANES:])
    out = jnp.concatenate([jnp.where(lane < 64, outs[0], outs[1]), jnp.where(lane < 64, outs[2], outs[3])], axis=-1)
    o_ref[0] = out.astype(BF16)


def _na_call(q, k, v, bias, n_lat, layer):
    B, H, T, _ = q.shape
    tq = NA_GROUP_ROWS * GRID_W
    win = NA_WIN_ROWS * GRID_W
    n_groups = n_lat // tq

    def variant(g):
        return 3 * layer + jnp.where(g == 0, 0, jnp.where(g == n_groups - 1, 2, 1))

    return pl.pallas_call(
        functools.partial(_na_kernel, n_lat=n_lat, win=win),
        grid=(B, n_groups),
        in_specs=[
            pl.BlockSpec((1, H, tq, HEAD_PAD), lambda b, g: (b, 0, g, 0)),
            pl.BlockSpec((1, H, T, HEAD_PAD), lambda b, g: (b, 0, 0, 0)),
            pl.BlockSpec((1, H, T, V_PAD), lambda b, g: (b, 0, 0, 0)),
            pl.BlockSpec((1, H, tq, win), lambda b, g: (variant(g), 0, 0, 0)),
        ],
        out_specs=pl.BlockSpec((1, tq, H * 64), lambda b, g: (b, g, 0)),
        out_shape=jax.ShapeDtypeStruct((B, n_lat, H * 64), BF16),
        compiler_params=_cparams(("parallel", "arbitrary")),
        name="na_attn",
    )(q, k, v, bias)


def _merge_kernel(x_ref, mod_ref, ya_ref, yb_ref, yc_ref, yd_ref, wg_ref, wb_ref, wo_ref, lg_ref, lb_ref, o_ref):
    D = D_MODEL
    x = x_ref[0]
    mod = mod_ref[0]
    sh, sc, gate_res = mod[:, 0:D], mod[:, D:2 * D], mod[:, 2 * D:3 * D]
    h = (x * (1.0 + sc) + sh).astype(BF16)
    acc = None
    for i, y_ref in enumerate((ya_ref, yb_ref, yc_ref, yd_ref)):
        gate = _sigmoid(_bdot(h, wg_ref[0, :, i * D:(i + 1) * D]))
        term = gate * _bdot(y_ref[0], wb_ref[0, i])
        acc = term if acc is None else acc + term
    out = _bdot(acc.astype(BF16), wo_ref[0])
    o_ref[0] = _layer_norm(DEEPNORM_ALPHA * x + gate_res * out, lg_ref[...], lb_ref[...])


def _merge_call(x, mod, ys, wg, wb, wo, lg, lb, tm, layer):
    B, S, D = x.shape
    per_batch_mod = mod.shape[0] > 1
    tok = lambda w: pl.BlockSpec((1, tm, w), lambda b, i: (b, i, 0))
    return pl.pallas_call(
        _merge_kernel,
        grid=(B, S // tm),
        in_specs=[tok(D), pl.BlockSpec((1, 1, 6 * D), lambda b, i: (b if per_batch_mod else 0, 0, 0))]
        + [tok(BRANCH_W)] * 4
        + [_layer_spec(wg, layer), _layer_spec(wb, layer), _layer_spec(wo, layer), _const_spec(lg.shape), _const_spec(lb.shape)],
        out_specs=tok(D),
        out_shape=jax.ShapeDtypeStruct((B, S, D), F32),
        compiler_params=_cparams(("parallel", "arbitrary")),
        name="merge_ln",
    )(x, mod, *ys, wg, wb, wo, lg, lb)


def _ffn_kernel(x_ref, mod_ref, wgu_ref, wd_ref, lg_ref, lb_ref, o_ref):
    D = D_MODEL
    x = x_ref[0]
    mod = mod_ref[0]
    sh, sc, gate_res = mod[:, 3 * D:4 * D], mod[:, 4 * D:5 * D], mod[:, 5 * D:6 * D]
    h = (x * (1.0 + sc) + sh).astype(BF16)
    gu = _bdot(h, wgu_ref[0])
    g, u = gu[:, :D_FF], gu[:, D_FF:]
    act = (g * _sigmoid(g) * u).astype(BF16)
    down = _bdot(act, wd_ref[0])
    o_ref[0] = _layer_norm(DEEPNORM_ALPHA * x + gate_res * down, lg_ref[...], lb_ref[...])


def _ffn_call(x, mod, wgu, wd, lg, lb, tm, layer):
    B, S, D = x.shape
    per_batch_mod = mod.shape[0] > 1
    tok = pl.BlockSpec((1, tm, D), lambda b, i: (b, i, 0))
    return pl.pallas_call(
        _ffn_kernel,
        grid=(B, S // tm),
        in_specs=[tok, pl.BlockSpec((1, 1, 6 * D), lambda b, i: (b if per_batch_mod else 0, 0, 0)),
                  _layer_spec(wgu, layer), _layer_spec(wd, layer), _const_spec(lg.shape), _const_spec(lb.shape)],
        out_specs=tok,
        out_shape=jax.ShapeDtypeStruct((B, S, D), F32),
        compiler_params=_cparams(("parallel", "arbitrary")),
        name="ffn_ln",
    )(x, mod, wgu, wd, lg, lb)


def kernel(x, c, ctx, c_ctx, w_ada, b_ada, w_in, g_q_a, w_q_up, g_kv_a, w_kv_up, rpb, lam_q1, lam_k1, lam_q2, lam_k2,
           g_sub, g_qn, g_kn, w_branch, w_out, ln1_g, ln1_b, w_gate_up, w_down, ln2_g, ln2_b):
    B, S, D = x.shape
    rows = S // GRID_W
    n_q = S // Q_TILE
    T = S + CTX_LEN
    ctx_blk = S // CTX_LEN

    c8 = jnp.concatenate([c, c_ctx[None, :], jnp.zeros((8 - B - 1, D), F32)], axis=0)
    mod = _ada_call(c8, w_ada, b_ada)
    tabs = _rope_tables(S)
    w_in_bf = w_in.astype(BF16)
    wsm = _small_weights(w_in_bf)
    wg = w_in_bf[..., GATE_OFF:]
    wqup = _pad_heads(w_q_up, MLA_HEADS, MLA_NOPE + MLA_ROPE).astype(BF16)
    wkvk, wkvv = _kv_up_weights(w_kv_up)
    wb, wo = w_branch.astype(BF16), w_out.astype(BF16)
    wgu, wd = w_gate_up.astype(BF16), w_down.astype(BF16)
    bias = _na_bias_tables(rpb, rows)
    row = lambda v: v.reshape(1, -1)
    pad64 = lambda v: jnp.concatenate([v, jnp.zeros_like(v)]).reshape(1, LANES)

    xl, xc = x, ctx
    for l in range(DEPTH):
        need_ctx = l < DEPTH - 1
        lam_init = 0.8 - 0.6 * math.exp(-0.3 * l)
        mod_l = mod[l, :B].reshape(B, 1, 6 * D)
        mod_c = mod[l, B:B + 1].reshape(1, 1, 6 * D)
        (qa, ka, vat, qn, kn, vn, qd, kd, vdt, qg, kg, vgt) = _proj_call(
            xl, xc, mod_l, mod_c, wsm, row(g_q_a[l]), wqup, row(g_kv_a[l]), wkvk, wkvv,
            pad64(g_qn[l]), pad64(g_kn[l]), tabs, layer=l)

        diff_args = (lam_q1[l].reshape(2, 2, DIFF_QK), lam_k1[l].reshape(2, 2, DIFF_QK),
                     lam_q2[l].reshape(2, 2, DIFF_QK), lam_k2[l].reshape(2, 2, DIFF_QK),
                     g_sub[l].reshape(V_DIM, 1))

        lat = lambda tq: dict(tq=tq, q_blk0=0, n_q=S // tq, t_blk=T, k_blk0=0)
        ya = _attn_call(qa, ka, vat, group=1, n_kv=2, name="mla_attn", **lat(ATTN_TILE))
        yb = _na_call(qn, kn, vn, bias, S, layer=l)
        yc = _attn_call(qd, kd, vdt, group=2, n_kv=2, diff_args=diff_args, lam_init=lam_init, name="diff_attn",
                        **lat(ATTN_TILE // 2))
        yd = _attn_call(qg, kg, vgt, group=2, n_kv=1, name="gqa_attn", **lat(ATTN_TILE))

        ln1 = (row(ln1_g[l]), row(ln1_b[l]))
        ln2 = (row(ln2_g[l]), row(ln2_b[l]))

        if need_ctx:
            cx = dict(tq=CTX_LEN, q_blk0=ctx_blk, n_q=1, t_blk=CTX_LEN, k_blk0=ctx_blk)
            ya_c = _attn_call(qa, ka, vat, group=1, n_kv=2, name="mla_attn_ctx", **cx)
            ctx0 = dict(tq=CTX_LEN, q_blk0=0, n_q=1, t_blk=CTX_LEN, k_blk0=0)
            yb_c = _attn_call(qn[:, :, S:], kn[:, :, S:], _transpose_v(vn[:, :, S:]), group=1, n_kv=2,
                              name="na_attn_ctx", **ctx0)
            yc_c = _attn_call(qd, kd, vdt, group=2, n_kv=2, diff_args=diff_args, lam_init=lam_init,
                              name="diff_attn_ctx", **cx)
            yd_c = _attn_call(qg, kg, vgt, group=2, n_kv=1, name="gqa_attn_ctx", **cx)
            xc1 = _merge_call(xc, mod_c, (ya_c, yb_c, yc_c, yd_c), wg, wb, wo, *ln1, tm=CTX_LEN, layer=l)
            xc = _ffn_call(xc1, mod_c, wgu, wd, *ln2, tm=CTX_LEN, layer=l)

        xl1 = _merge_call(xl, mod_l, (ya, yb, yc, yd), wg, wb, wo, *ln1, tm=Q_TILE, layer=l)
        xl = _ffn_call(xl1, mod_l, wgu, wd, *ln2, tm=Q_TILE, layer=l)
    return xl
```

```python
import functools
import math

import jax
import jax.numpy as jnp
import numpy as np
from jax import lax
from jax.experimental import pallas as pl
from jax.experimental.pallas import tpu as pltpu

F32 = jnp.float32
BF16 = jnp.bfloat16

D_MODEL = 1024
DEPTH = 2
CTX_LEN = 256
GRID_W = 64
ROPE_THETA = 10000.0
EPS = 1e-6

MLA_HEADS = 4
MLA_Q_RANK = 384
MLA_KV_RANK = 256
MLA_NOPE = 64
MLA_ROPE = 32
MLA_V = 64
NA_HEADS = 4
NA_DIM = 64
NA_ROWS = 8
NA_COLS = 16
DIFF_HEADS = 4
DIFF_QK = 32
DIFF_V = 64
GQA_HEADS = 4
GQA_KV_HEADS = 2
GQA_DIM = 64
N_BRANCH = 4
BRANCH_W = 256
D_FF = -(-8 * D_MODEL // (3 * 256)) * 256
DEEPNORM_ALPHA = (2 * DEPTH) ** 0.25

IN_SIZES = (
    MLA_Q_RANK, MLA_KV_RANK, MLA_ROPE,
    NA_HEADS * NA_DIM, NA_HEADS * NA_DIM, NA_HEADS * NA_DIM,
    DIFF_HEADS * 2 * DIFF_QK, DIFF_HEADS * 2 * DIFF_QK, DIFF_HEADS * DIFF_V,
    GQA_HEADS * GQA_DIM, GQA_KV_HEADS * GQA_DIM, GQA_KV_HEADS * GQA_DIM,
    N_BRANCH * D_MODEL,
)
IN_OFFS = tuple(int(v) for v in np.concatenate([[0], np.cumsum(IN_SIZES)]))
GATE_OFF = IN_OFFS[12]

LANES = 128
HEAD_PAD = LANES
V_PAD = 2 * LANES
TOK_TILE = 256
Q_TILE = 1024
STREAM_ROWS = 256
ATTN_TILE = 512
NA_GROUP_ROWS = 4
NA_WIN_ROWS = 12
SCORE_COLS = 256
MAX_CHAINS = 8
KEY_TILE = 256
KEY_CHUNKS = 4
V_DIM = 64
VT_ROWS = 80
MASK_VALUE = -1e30
VMEM_LIMIT = 56 * 1024 * 1024

LOG2E = math.log2(math.e)
MLA_SCALE = (MLA_NOPE + MLA_ROPE) ** -0.5 * LOG2E
NA_SCALE = NA_DIM ** -0.5 * LOG2E
DIFF_SCALE = DIFF_QK ** -0.5 * LOG2E
GQA_SCALE = GQA_DIM ** -0.5 * LOG2E

P_CQ = 0
P_CKV = P_CQ + MLA_Q_RANK
P_KPE = P_CKV + MLA_KV_RANK
P_QN = P_KPE + HEAD_PAD
P_KN = P_QN + NA_HEADS * HEAD_PAD
P_VN = P_KN + NA_HEADS * HEAD_PAD
P_QD = P_VN + NA_HEADS * HEAD_PAD
P_KD = P_QD + DIFF_HEADS * HEAD_PAD
P_VD = P_KD + DIFF_HEADS * HEAD_PAD
P_QG = P_VD + DIFF_HEADS * HEAD_PAD
P_KG = P_QG + GQA_HEADS * HEAD_PAD
P_VG = P_KG + GQA_KV_HEADS * HEAD_PAD
P_END = P_VG + GQA_KV_HEADS * HEAD_PAD

NT_DIMS = (((1,), (1,)), ((), ()))


def _cparams(sem):
    return pltpu.CompilerParams(dimension_semantics=sem, vmem_limit_bytes=VMEM_LIMIT)


def _const_spec(shape):
    nd = len(shape)
    return pl.BlockSpec(shape, lambda *_: (0,) * nd, pipeline_mode=pl.Buffered(1))


def _layer_spec(w, layer):
    nd = w.ndim - 1
    return pl.BlockSpec((1,) + w.shape[1:], lambda *_: (layer,) + (0,) * nd, pipeline_mode=pl.Buffered(1))


def _pad_heads(w, n_heads, width, dup=False):
    lead = w.shape[:-1]
    h = w.reshape(lead + (n_heads, width))
    fill = h if dup else jnp.zeros(lead + (n_heads, HEAD_PAD - width), w.dtype)
    return jnp.concatenate([h, fill], axis=-1).reshape(lead + (n_heads * HEAD_PAD,))


def _small_weights(w_in):
    o = IN_OFFS
    seg = lambda i: w_in[..., o[i]:o[i + 1]]
    zeros = lambda n: jnp.zeros(w_in.shape[:-1] + (n,), w_in.dtype)
    parts = [seg(0), seg(1), zeros(MLA_NOPE), seg(2), zeros(HEAD_PAD - MLA_NOPE - MLA_ROPE),
             _pad_heads(seg(3), NA_HEADS, 64), _pad_heads(seg(4), NA_HEADS, 64), _pad_heads(seg(5), NA_HEADS, 64, True),
             _pad_heads(seg(6), DIFF_HEADS, 64), _pad_heads(seg(7), DIFF_HEADS, 64), _pad_heads(seg(8), DIFF_HEADS, 64, True),
             _pad_heads(seg(9), GQA_HEADS, 64), _pad_heads(seg(10), GQA_KV_HEADS, 64),
             _pad_heads(seg(11), GQA_KV_HEADS, 64, True)]
    out = jnp.concatenate(parts, axis=-1)
    assert out.shape[-1] == P_END
    return out.astype(BF16)


def _kv_up_weights(w_kv_up):
    lead = w_kv_up.shape[:-1]
    h = w_kv_up.reshape(lead + (MLA_HEADS, MLA_NOPE + MLA_V))
    k = _pad_heads(h[..., :MLA_NOPE].reshape(lead + (-1,)), MLA_HEADS, MLA_NOPE)
    v = _pad_heads(h[..., MLA_NOPE:].reshape(lead + (-1,)), MLA_HEADS, MLA_V, True)
    return k.astype(BF16), v.astype(BF16)


def _rope_pattern(S, rot_dim):
    t = jnp.arange(S)
    pos = jnp.stack([t // GRID_W, t % GRID_W], axis=-1).astype(F32)
    n_f = rot_dim // 4
    inv = ROPE_THETA ** (-jnp.arange(n_f, dtype=F32) / n_f)
    ang = pos[:, :, None] * inv
    cos, sin = jnp.cos(ang), jnp.sin(ang)
    c = jnp.concatenate([cos[:, 0], cos[:, 0], cos[:, 1], cos[:, 1]], axis=-1)
    s = jnp.concatenate([-sin[:, 0], sin[:, 0], -sin[:, 1], sin[:, 1]], axis=-1)
    c = jnp.concatenate([c, jnp.ones((CTX_LEN, rot_dim), F32)], axis=0)
    s = jnp.concatenate([s, jnp.zeros((CTX_LEN, rot_dim), F32)], axis=0)
    return c, s


def _rope_tables(S):
    T = S + CTX_LEN
    c32, s32 = _rope_pattern(S, 32)
    c64, s64 = _rope_pattern(S, 64)
    one = lambda n: jnp.ones((T, n), F32)
    zero = lambda n: jnp.zeros((T, n), F32)
    cat = lambda xs: jnp.concatenate(xs, axis=-1)
    ca, sa = cat([one(64), c32, one(32)]), cat([zero(64), s32, zero(32)])
    cd, sd = cat([c32, c32, one(64)]), cat([s32, s32, zero(64)])
    cg, sg = cat([c64, one(64)]), cat([s64, zero(64)])
    return ca, sa, cd, sd, cg, sg


def _na_bias_tables(rpb, rows):
    depth, n_heads = rpb.shape[:2]
    n_groups = rows // NA_GROUP_ROWS
    kh, kw = min(NA_ROWS, rows), NA_COLS
    w = np.arange(GRID_W)[:, None]
    c = np.arange(GRID_W)[None, :]
    cs = np.clip(w - kw // 2, 0, GRID_W - kw)
    col_valid = (c >= cs) & (c < cs + kw)
    col_off = c - w + (NA_COLS - 1)
    onehot_c = ((col_off[None] == np.arange(2 * NA_COLS - 1)[:, None, None]) & col_valid[None]).astype(np.float32)
    a = np.arange(NA_GROUP_ROWS)[:, None]
    i = np.arange(NA_WIN_ROWS)[None, :]
    out = []
    for g in (0, 1, n_groups - 1):
        r = NA_GROUP_ROWS * g + a
        i_abs = np.clip(NA_GROUP_ROWS * g - kh // 2, 0, rows - NA_WIN_ROWS) + i
        rs = np.clip(r - kh // 2, 0, rows - kh)
        row_valid = (i_abs >= rs) & (i_abs < rs + kh)
        row_off = i_abs - r + (NA_ROWS - 1)
        onehot_r = ((row_off[..., None] == np.arange(2 * NA_ROWS - 1)) & row_valid[..., None]).astype(np.float32)
        valid = row_valid[:, None, :, None] & col_valid[None, :, None, :]
        b = jnp.einsum('air,lhrk,kwc->lhawic', jnp.asarray(onehot_r), rpb.astype(F32), jnp.asarray(onehot_c),
                       precision=lax.Precision.HIGHEST)
        b = jnp.where(jnp.asarray(valid)[None, None], b * LOG2E, MASK_VALUE)
        out.append(b.reshape(depth, n_heads, NA_GROUP_ROWS * GRID_W, NA_WIN_ROWS * GRID_W))
    return jnp.stack(out, axis=1).reshape((depth * 3,) + out[0].shape[1:])


def _sigmoid(x):
    return 1.0 / (1.0 + jnp.exp(-x))


def _rms(x, g, n):
    ms = jnp.sum(x * x, axis=-1, keepdims=True) * (1.0 / n)
    return x * lax.rsqrt(ms + EPS) * g


def _layer_norm(x, g, b):
    mu = jnp.mean(x, axis=-1, keepdims=True)
    xc = x - mu
    var = jnp.mean(xc * xc, axis=-1, keepdims=True)
    return xc * lax.rsqrt(var + EPS) * g + b


def _rope(z, c, s, n_f):
    lane = lax.broadcasted_iota(jnp.int32, z.shape, 1)
    first = (lane & n_f) == 0
    up = pltpu.roll(z, LANES - n_f, 1)
    dn = pltpu.roll(z, n_f, 1)
    return z * c + jnp.where(first, up, dn) * s


def _bdot(a, b):
    return jnp.dot(a, b, preferred_element_type=F32)


def _row_streams(n_rows):
    step = min(STREAM_ROWS, n_rows)
    return [slice(r, r + step) for r in range(0, n_rows, step)]


def _ada_kernel(c_ref, w_ref, b_ref, o_ref):
    c = c_ref[...]
    a = c * _sigmoid(c)
    o_ref[0] = jnp.dot(a, w_ref[0], preferred_element_type=F32, precision=lax.Precision.HIGHEST) + b_ref[0]


def _ada_call(c8, w_ada, b_ada):
    L, D, N = w_ada.shape
    tn = 1536
    return pl.pallas_call(
        _ada_kernel,
        grid=(L, N // tn),
        in_specs=[
            pl.BlockSpec((8, D), lambda l, n: (0, 0)),
            pl.BlockSpec((1, D, tn), lambda l, n: (l, 0, n)),
            pl.BlockSpec((1, 1, tn), lambda l, n: (l, 0, n)),
        ],
        out_specs=pl.BlockSpec((1, 8, tn), lambda l, n: (l, 0, n)),
        out_shape=jax.ShapeDtypeStruct((L, 8, N), F32),
        compiler_params=_cparams(("parallel", "parallel")),
        name="ada_mod",
    )(c8, w_ada, b_ada.reshape(L, 1, N))


def _proj_kernel(x_ref, ctx_ref, modl_ref, modc_ref, w_ref, gqa_ref, wqup_ref, gkva_ref, wkvk_ref, wkvv_ref,
                 gqn_ref, gkn_ref, ca_ref, sa_ref, cd_ref, sd_ref, cg_ref, sg_ref,
                 qa_ref, ka_ref, va_ref, qn_ref, kn_ref, vn_ref, qd_ref, kd_ref, vd_ref, qg_ref, kg_ref, vg_ref,
                 *, n_lat_tiles):
    D = D_MODEL
    is_ctx = pl.program_id(0) == n_lat_tiles
    x = jnp.where(is_ctx, ctx_ref[0], x_ref[0])
    ml, mc = modl_ref[0], modc_ref[0]
    sh = jnp.where(is_ctx, mc[:, 0:D], ml[:, 0:D])
    sc = jnp.where(is_ctx, mc[:, D:2 * D], ml[:, D:2 * D])
    h = (x * (1.0 + sc) + sh).astype(BF16)
    z = _bdot(h, w_ref[0])
    tm = z.shape[0]
    ones = jnp.ones((tm, LANES), F32)
    lane = lax.broadcasted_iota(jnp.int32, (tm, LANES), 1)
    head = lambda base, i: z[:, base + HEAD_PAD * i: base + HEAD_PAD * (i + 1)]
    with_ones = lambda v: jnp.concatenate([v, ones], axis=-1).astype(BF16)
    tail = (lax.broadcasted_iota(jnp.int32, (VT_ROWS - V_DIM, tm), 0) == 0).astype(F32)
    transposed = lambda v: jnp.concatenate([v.T[:V_DIM], tail], axis=0).astype(BF16)

    ca, sa = ca_ref[...], sa_ref[...]
    cqn = _rms(z[:, P_CQ:P_CKV], gqa_ref[...], MLA_Q_RANK).astype(BF16)
    qa_full = _bdot(cqn, wqup_ref[0])
    ckvn = _rms(z[:, P_CKV:P_KPE], gkva_ref[...], MLA_KV_RANK).astype(BF16)
    k_full = _bdot(ckvn, wkvk_ref[0])
    v_full = _bdot(ckvn, wkvv_ref[0])
    kpe = _rope(z[:, P_KPE:P_QN], ca, sa, MLA_ROPE // 4)
    for i in range(MLA_HEADS):
        sl = slice(HEAD_PAD * i, HEAD_PAD * (i + 1))
        qa_ref[0, i] = (_rope(qa_full[:, sl], ca, sa, MLA_ROPE // 4) * MLA_SCALE).astype(BF16)
        ka_ref[0, i] = (k_full[:, sl] + kpe).astype(BF16)
        va_ref[0, i] = transposed(v_full[:, sl])

    for i in range(NA_HEADS):
        qn_ref[0, i] = (head(P_QN, i) * NA_SCALE).astype(BF16)
        kn_ref[0, i] = head(P_KN, i).astype(BF16)
        vn_ref[0, i] = with_ones(head(P_VN, i))

    cd, sd = cd_ref[...], sd_ref[...]
    for i in range(DIFF_HEADS):
        zq = _rope(head(P_QD, i), cd, sd, DIFF_QK // 4) * DIFF_SCALE
        qd_ref[0, 2 * i] = jnp.where(lane < DIFF_QK, zq, 0.0).astype(BF16)
        qd_ref[0, 2 * i + 1] = jnp.where(lane >= DIFF_QK, zq, 0.0).astype(BF16)
        kd_ref[0, i] = _rope(head(P_KD, i), cd, sd, DIFF_QK // 4).astype(BF16)
        vd_ref[0, i] = transposed(head(P_VD, i))

    cg, sg = cg_ref[...], sg_ref[...]
    gqn, gkn = gqn_ref[...], gkn_ref[...]
    for i in range(GQA_HEADS):
        qn = _rms(head(P_QG, i), gqn, GQA_DIM)
        qg_ref[0, i] = (_rope(qn, cg, sg, GQA_DIM // 4) * GQA_SCALE).astype(BF16)
    for i in range(GQA_KV_HEADS):
        kn = _rms(head(P_KG, i), gkn, GQA_DIM)
        kg_ref[0, i] = _rope(kn, cg, sg, GQA_DIM // 4).astype(BF16)
        vg_ref[0, i] = transposed(head(P_VG, i))


def _proj_call(xl, xc, mod_l, mod_c, wsm, gqa, wqup, gkva, wkvk, wkvv, gqn, gkn, tabs, layer):
    B, S, D = xl.shape
    tm = TOK_TILE
    n_lat = S // tm
    T = S + CTX_LEN
    grid = (n_lat + 1, B)
    tab_spec = pl.BlockSpec((tm, LANES), lambda t, b: (t, 0))
    in_specs = [
        pl.BlockSpec((1, tm, D), lambda t, b: (b, jnp.minimum(t, n_lat - 1), 0)),
        pl.BlockSpec((1, tm, D), lambda t, b: (b, 0, 0)),
        pl.BlockSpec((1, 1, 6 * D), lambda t, b: (b, 0, 0)),
        pl.BlockSpec((1, 1, 6 * D), lambda t, b: (0, 0, 0)),
        _layer_spec(wsm, layer), _const_spec(gqa.shape), _layer_spec(wqup, layer), _const_spec(gkva.shape),
        _layer_spec(wkvk, layer), _layer_spec(wkvv, layer), _const_spec(gqn.shape), _const_spec(gkn.shape),
    ] + [tab_spec] * 6

    def o(n_heads, width):
        return (jax.ShapeDtypeStruct((B, n_heads, T, width), BF16),
                pl.BlockSpec((1, n_heads, tm, width), lambda t, b: (b, 0, t, 0)))

    def ot(n_heads):
        return (jax.ShapeDtypeStruct((B, n_heads, VT_ROWS, T), BF16),
                pl.BlockSpec((1, n_heads, VT_ROWS, tm), lambda t, b: (b, 0, 0, t)))

    outs = [o(MLA_HEADS, HEAD_PAD), o(MLA_HEADS, HEAD_PAD), ot(MLA_HEADS),
            o(NA_HEADS, HEAD_PAD), o(NA_HEADS, HEAD_PAD), o(NA_HEADS, V_PAD),
            o(2 * DIFF_HEADS, HEAD_PAD), o(DIFF_HEADS, HEAD_PAD), ot(DIFF_HEADS),
            o(GQA_HEADS, HEAD_PAD), o(GQA_KV_HEADS, HEAD_PAD), ot(GQA_KV_HEADS)]
    return pl.pallas_call(
        functools.partial(_proj_kernel, n_lat_tiles=n_lat),
        grid=grid,
        in_specs=in_specs,
        out_specs=[s for _, s in outs],
        out_shape=[s for s, _ in outs],
        compiler_params=_cparams(("arbitrary", "arbitrary")),
        name="in_proj",
    )(xl, xc, mod_l, mod_c, wsm, gqa, wqup, gkva, wkvk, wkvv, gqn, gkn, *tabs)


def _key_chunks(t_keys):
    n_tiles = t_keys // KEY_TILE
    n_chunks = min(KEY_CHUNKS, n_tiles)
    sizes = [n_tiles // n_chunks + (1 if c < n_tiles % n_chunks else 0) for c in range(n_chunks)]
    bounds = np.cumsum([0] + sizes) * KEY_TILE
    return [(int(bounds[c]), int(bounds[c + 1])) for c in range(n_chunks)]


def _attn_kernel(*refs, n_kv, group, diff, lam_init):
    q_ref, k_ref, v_ref = refs[:3]
    s_ref, ma_ref, mb_ref, pa_ref, pb_ref = refs[-5:]
    step = pl.program_id(0)
    tq = q_ref.shape[2]
    n_dots = s_ref.shape[0]
    cols = s_ref.shape[2]
    split = n_dots // n_kv

    @pl.when(step == 0)
    def _():
        s_ref[...] = jnp.zeros(s_ref.shape, F32)
        mb_ref[...] = jnp.zeros(mb_ref.shape, F32)
        pb_ref[...] = jnp.ones(pb_ref.shape, BF16)

    def body(m_write, m_read, p_write, p_read):
        qs = []
        for j in range(n_kv):
            q = q_ref[0, j * group:(j + 1) * group].reshape(group * tq, HEAD_PAD)
            qs += [q[r * cols:(r + 1) * cols] for r in range(split)]
        acc = [None] * n_dots
        run_max = [None] * n_dots
        for lo, hi in _key_chunks(k_ref.shape[2]):
            for d in range(n_dots):
                j = d // split
                part = _bdot(v_ref[0, j, :, lo:hi], p_read[d, lo:hi, :])
                acc[d] = part if acc[d] is None else acc[d] + part
                p_write[d, lo:hi, :] = jnp.exp2(s_ref[d, lo:hi, :] - m_read[d]).astype(BF16)
                s = lax.dot_general(k_ref[0, j, lo:hi, :], qs[d], NT_DIMS, preferred_element_type=F32)
                s_ref[d, lo:hi, :] = s
                pm = jnp.max(s.reshape(MAX_CHAINS, (hi - lo) // MAX_CHAINS, cols), axis=1)
                run_max[d] = pm if run_max[d] is None else jnp.maximum(run_max[d], pm)
        heads = []
        for j in range(n_kv):
            outs = []
            for d in range(j * split, (j + 1) * split):
                m_write[d] = jnp.max(run_max[d], axis=0, keepdims=True)
                outs.append(acc[d][:V_DIM] / acc[d][V_DIM:V_DIM + 1])
            on = outs[0] if split == 1 else jnp.concatenate(outs, axis=1)
            heads += [on[:, g * tq:(g + 1) * tq] for g in range(group)]
        _attn_finish(heads, refs, diff, lam_init)

    parity = lax.rem(step, 2)

    @pl.when(parity == 0)
    def _():
        body(ma_ref, mb_ref, pa_ref, pb_ref)

    @pl.when(parity == 1)
    def _():
        body(mb_ref, ma_ref, pb_ref, pa_ref)


def _attn_finish(heads, refs, diff, lam_init):
    if diff:
        lq1_ref, lk1_ref, lq2_ref, lk2_ref, gsub_ref, o_ref = refs[3:9]
    else:
        o_ref = refs[3]
    if diff:
        ys = []
        for j in range(2):
            dot_l = lambda a, b: jnp.sum(a[0, j:j + 1] * b[0, j:j + 1], axis=-1, keepdims=True)
            lam = jnp.exp(dot_l(lq1_ref, lk1_ref)) - jnp.exp(dot_l(lq2_ref, lk2_ref)) + lam_init
            y = heads[2 * j] - lam * heads[2 * j + 1]
            ms = jnp.mean(y * y, axis=0, keepdims=True)
            ys.append(y * lax.rsqrt(ms + EPS) * gsub_ref[...] * (1.0 - lam_init))
        heads = ys
    o_ref[0] = jnp.concatenate(heads, axis=0).T.astype(BF16)


def _transpose_v(v):
    B, H, T, _ = v.shape
    vt = jnp.swapaxes(v[..., :V_DIM], -1, -2)
    tail = jnp.zeros((B, H, VT_ROWS - V_DIM, T), v.dtype).at[:, :, 0, :].set(1.0)
    return jnp.concatenate([vt, tail], axis=2)


def _attn_call(q, k, vt, *, group, n_kv, tq, q_blk0, n_q, t_blk, k_blk0, diff_args=None, lam_init=0.0, name):
    B, Hq = q.shape[0], q.shape[1]
    n_pairs = Hq // (n_kv * group)
    qh = n_kv * group
    n_units = B * n_pairs * n_q
    n_dots = qh * tq // SCORE_COLS

    def unit(u):
        return u // (n_pairs * n_q), (u // n_q) % n_pairs, u % n_q

    def stage1(s):
        return unit(jnp.minimum(s, n_units - 1))

    def stage3(s):
        return unit(jnp.maximum(s - 2, 0))

    def q_map(s):
        b, p, i = stage1(s)
        return b, p, q_blk0 + i, 0

    def k_map(s):
        b, p, _ = stage1(s)
        return b, p, k_blk0, 0

    def v_map(s):
        b, p, _ = stage3(s)
        return b, p, 0, k_blk0

    def o_map(s):
        b, p, i = stage3(s)
        return b, i, p

    in_specs = [
        pl.BlockSpec((1, qh, tq, HEAD_PAD), q_map),
        pl.BlockSpec((1, n_kv, t_blk, HEAD_PAD), k_map),
        pl.BlockSpec((1, n_kv, VT_ROWS, t_blk), v_map),
    ]
    args = [q, k, vt]
    if diff_args is not None:
        lam_spec = pl.BlockSpec((1, 2, DIFF_QK), lambda s: (stage3(s)[1], 0, 0))
        in_specs += [lam_spec] * 4 + [pl.BlockSpec((V_DIM, 1), lambda s: (0, 0))]
        args += list(diff_args)
    score_buf = pltpu.VMEM((n_dots, t_blk, SCORE_COLS), F32)
    prob_buf = pltpu.VMEM((n_dots, t_blk, SCORE_COLS), BF16)
    max_buf = pltpu.VMEM((n_dots, 1, SCORE_COLS), F32)
    return pl.pallas_call(
        functools.partial(_attn_kernel, n_kv=n_kv, group=group, diff=diff_args is not None, lam_init=lam_init),
        grid=(n_units + 2,),
        in_specs=in_specs,
        out_specs=pl.BlockSpec((1, tq, LANES), o_map),
        out_shape=jax.ShapeDtypeStruct((B, n_q * tq, n_pairs * LANES), BF16),
        scratch_shapes=[score_buf, max_buf, max_buf, prob_buf, prob_buf],
        compiler_params=_cparams(("arbitrary",)),
        name=name,
    )(*args)


def _na_kernel(q_ref, k_ref, v_ref, bias_ref, o_ref, *, n_lat, win):
    g = pl.program_id(1)
    rows = n_lat // GRID_W
    start = jnp.clip(NA_GROUP_ROWS * g - NA_ROWS // 2, 0, rows - NA_WIN_ROWS) * GRID_W
    start = pl.multiple_of(start, GRID_W)
    tq = q_ref.shape[2]
    lane = lax.broadcasted_iota(jnp.int32, (tq, LANES), 1)
    outs = []
    for h in range(NA_HEADS):
        q = q_ref[0, h]
        s_l = lax.dot_general(q, k_ref[0, h, pl.ds(start, win), :], NT_DIMS, preferred_element_type=F32)
        s_l = s_l + bias_ref[0, h]
        s_c = lax.dot_general(q, k_ref[0, h, n_lat:, :], NT_DIMS, preferred_element_type=F32)
        m = jnp.maximum(jnp.max(s_l, axis=-1, keepdims=True), jnp.max(s_c, axis=-1, keepdims=True))
        p_l = jnp.exp2(s_l - m).astype(BF16)
        p_c = jnp.exp2(s_c - m).astype(BF16)
        oe = _bdot(p_l, v_ref[0, h, pl.ds(start, win), :]) + _bdot(p_c, v_ref[0, h, n_lat:, :])
        outs.append(oe[:, :LANES] / oe[:, LANES:])
    out = jnp.concatenate([jnp.where(lane < 64, outs[0], outs[1]), jnp.where(lane < 64, outs[2], outs[3])], axis=-1)
    o_ref[0] = out.astype(BF16)


def _na_call(q, k, v, bias, n_lat, layer):
    B, H, T, _ = q.shape
    tq = NA_GROUP_ROWS * GRID_W
    win = NA_WIN_ROWS * GRID_W
    n_groups = n_lat // tq

    def variant(g):
        return 3 * layer + jnp.where(g == 0, 0, jnp.where(g == n_groups - 1, 2, 1))

    return pl.pallas_call(
        functools.partial(_na_kernel, n_lat=n_lat, win=win),
        grid=(B, n_groups),
        in_specs=[
            pl.BlockSpec((1, H, tq, HEAD_PAD), lambda b, g: (b, 0, g, 0)),
            pl.BlockSpec((1, H, T, HEAD_PAD), lambda b, g: (b, 0, 0, 0)),
            pl.BlockSpec((1, H, T, V_PAD), lambda b, g: (b, 0, 0, 0)),
            pl.BlockSpec((1, H, tq, win), lambda b, g: (variant(g), 0, 0, 0)),
        ],
        out_specs=pl.BlockSpec((1, tq, H * 64), lambda b, g: (b, g, 0)),
        out_shape=jax.ShapeDtypeStruct((B, n_lat, H * 64), BF16),
        compiler_params=_cparams(("parallel", "arbitrary")),
        name="na_attn",
    )(q, k, v, bias)


def _merge_kernel(x_ref, mod_ref, ya_ref, yb_ref, yc_ref, yd_ref, wg_ref, wb_ref, wo_ref, lg_ref, lb_ref, o_ref):
    D = D_MODEL
    mod = mod_ref[0]
    sh, sc, gate_res = mod[:, 0:D], mod[:, D:2 * D], mod[:, 2 * D:3 * D]
    for rows in _row_streams(x_ref.shape[1]):
        x = x_ref[0, rows]
        h = (x * (1.0 + sc) + sh).astype(BF16)
        acc = None
        for i, y_ref in enumerate((ya_ref, yb_ref, yc_ref, yd_ref)):
            gate = _sigmoid(_bdot(h, wg_ref[0, :, i * D:(i + 1) * D]))
            term = gate * _bdot(y_ref[0, rows], wb_ref[0, i])
            acc = term if acc is None else acc + term
        out = _bdot(acc.astype(BF16), wo_ref[0])
        o_ref[0, rows] = _layer_norm(DEEPNORM_ALPHA * x + gate_res * out, lg_ref[...], lb_ref[...])


def _merge_call(x, mod, ys, wg, wb, wo, lg, lb, tm, layer):
    B, S, D = x.shape
    per_batch_mod = mod.shape[0] > 1
    tok = lambda w: pl.BlockSpec((1, tm, w), lambda b, i: (b, i, 0))
    return pl.pallas_call(
        _merge_kernel,
        grid=(B, S // tm),
        in_specs=[tok(D), pl.BlockSpec((1, 1, 6 * D), lambda b, i: (b if per_batch_mod else 0, 0, 0))]
        + [tok(BRANCH_W)] * 4
        + [_layer_spec(wg, layer), _layer_spec(wb, layer), _layer_spec(wo, layer), _const_spec(lg.shape), _const_spec(lb.shape)],
        out_specs=tok(D),
        out_shape=jax.ShapeDtypeStruct((B, S, D), F32),
        compiler_params=_cparams(("parallel", "arbitrary")),
        name="merge_ln",
    )(x, mod, *ys, wg, wb, wo, lg, lb)


def _ffn_kernel(x_ref, mod_ref, wgu_ref, wd_ref, lg_ref, lb_ref, o_ref):
    D = D_MODEL
    mod = mod_ref[0]
    sh, sc, gate_res = mod[:, 3 * D:4 * D], mod[:, 4 * D:5 * D], mod[:, 5 * D:6 * D]
    for rows in _row_streams(x_ref.shape[1]):
        x = x_ref[0, rows]
        h = (x * (1.0 + sc) + sh).astype(BF16)
        gu = _bdot(h, wgu_ref[0])
        g, u = gu[:, :D_FF], gu[:, D_FF:]
        act = (g * _sigmoid(g) * u).astype(BF16)
        down = _bdot(act, wd_ref[0])
        o_ref[0, rows] = _layer_norm(DEEPNORM_ALPHA * x + gate_res * down, lg_ref[...], lb_ref[...])


def _ffn_call(x, mod, wgu, wd, lg, lb, tm, layer):
    B, S, D = x.shape
    per_batch_mod = mod.shape[0] > 1
    tok = pl.BlockSpec((1, tm, D), lambda b, i: (b, i, 0))
    return pl.pallas_call(
        _ffn_kernel,
        grid=(B, S // tm),
        in_specs=[tok, pl.BlockSpec((1, 1, 6 * D), lambda b, i: (b if per_batch_mod else 0, 0, 0)),
                  _layer_spec(wgu, layer), _layer_spec(wd, layer), _const_spec(lg.shape), _const_spec(lb.shape)],
        out_specs=tok,
        out_shape=jax.ShapeDtypeStruct((B, S, D), F32),
        compiler_params=_cparams(("parallel", "arbitrary")),
        name="ffn_ln",
    )(x, mod, wgu, wd, lg, lb)


def kernel(x, c, ctx, c_ctx, w_ada, b_ada, w_in, g_q_a, w_q_up, g_kv_a, w_kv_up, rpb, lam_q1, lam_k1, lam_q2, lam_k2,
           g_sub, g_qn, g_kn, w_branch, w_out, ln1_g, ln1_b, w_gate_up, w_down, ln2_g, ln2_b):
    B, S, D = x.shape
    rows = S // GRID_W
    n_q = S // Q_TILE
    T = S + CTX_LEN
    ctx_blk = S // CTX_LEN

    c8 = jnp.concatenate([c, c_ctx[None, :], jnp.zeros((8 - B - 1, D), F32)], axis=0)
    mod = _ada_call(c8, w_ada, b_ada)
    tabs = _rope_tables(S)
    w_in_bf = w_in.astype(BF16)
    wsm = _small_weights(w_in_bf)
    wg = w_in_bf[..., GATE_OFF:]
    wqup = _pad_heads(w_q_up, MLA_HEADS, MLA_NOPE + MLA_ROPE).astype(BF16)
    wkvk, wkvv = _kv_up_weights(w_kv_up)
    wb, wo = w_branch.astype(BF16), w_out.astype(BF16)
    wgu, wd = w_gate_up.astype(BF16), w_down.astype(BF16)
    bias = _na_bias_tables(rpb, rows)
    row = lambda v: v.reshape(1, -1)
    pad64 = lambda v: jnp.concatenate([v, jnp.zeros_like(v)]).reshape(1, LANES)

    xl, xc = x, ctx
    for l in range(DEPTH):
        need_ctx = l < DEPTH - 1
        lam_init = 0.8 - 0.6 * math.exp(-0.3 * l)
        mod_l = mod[l, :B].reshape(B, 1, 6 * D)
        mod_c = mod[l, B:B + 1].reshape(1, 1, 6 * D)
        (qa, ka, vat, qn, kn, vn, qd, kd, vdt, qg, kg, vgt) = _proj_call(
            xl, xc, mod_l, mod_c, wsm, row(g_q_a[l]), wqup, row(g_kv_a[l]), wkvk, wkvv,
            pad64(g_qn[l]), pad64(g_kn[l]), tabs, layer=l)

        diff_args = (lam_q1[l].reshape(2, 2, DIFF_QK), lam_k1[l].reshape(2, 2, DIFF_QK),
                     lam_q2[l].reshape(2, 2, DIFF_QK), lam_k2[l].reshape(2, 2, DIFF_QK),
                     g_sub[l].reshape(V_DIM, 1))

        lat = lambda tq: dict(tq=tq, q_blk0=0, n_q=S // tq, t_blk=T, k_blk0=0)
        ya = _attn_call(qa, ka, vat, group=1, n_kv=2, name="mla_attn", **lat(ATTN_TILE))
        yb = _na_call(qn, kn, vn, bias, S, layer=l)
        yc = _attn_call(qd, kd, vdt, group=2, n_kv=2, diff_args=diff_args, lam_init=lam_init, name="diff_attn",
                        **lat(ATTN_TILE // 2))
        yd = _attn_call(qg, kg, vgt, group=2, n_kv=1, name="gqa_attn", **lat(ATTN_TILE))

        ln1 = (row(ln1_g[l]), row(ln1_b[l]))
        ln2 = (row(ln2_g[l]), row(ln2_b[l]))

        if need_ctx:
            cx = dict(tq=CTX_LEN, q_blk0=ctx_blk, n_q=1, t_blk=CTX_LEN, k_blk0=ctx_blk)
            ya_c = _attn_call(qa, ka, vat, group=1, n_kv=2, name="mla_attn_ctx", **cx)
            ctx0 = dict(tq=CTX_LEN, q_blk0=0, n_q=1, t_blk=CTX_LEN, k_blk0=0)
            yb_c = _attn_call(qn[:, :, S:], kn[:, :, S:], _transpose_v(vn[:, :, S:]), group=1, n_kv=2,
                              name="na_attn_ctx", **ctx0)
            yc_c = _attn_call(qd, kd, vdt, group=2, n_kv=2, diff_args=diff_args, lam_init=lam_init,
                              name="diff_attn_ctx", **cx)
            yd_c = _attn_call(qg, kg, vgt, group=2, n_kv=1, name="gqa_attn_ctx", **cx)
            xc1 = _merge_call(xc, mod_c, (ya_c, yb_c, yc_c, yd_c), wg, wb, wo, *ln1, tm=CTX_LEN, layer=l)
            xc = _ffn_call(xc1, mod_c, wgu, wd, *ln2, tm=CTX_LEN, layer=l)

        xl1 = _merge_call(xl, mod_l, (ya, yb, yc, yd), wg, wb, wo, *ln1, tm=Q_TILE, layer=l)
        xl = _ffn_call(xl1, mod_l, wgu, wd, *ln2, tm=Q_TILE, layer=l)
    return xl
```

```python
import functools
import math

import jax
import jax.numpy as jnp
import numpy as np
from jax import lax
from jax.experimental import pallas as pl
from jax.experimental.pallas import tpu as pltpu

F32 = jnp.float32
BF16 = jnp.bfloat16

D_MODEL = 1024
DEPTH = 2
CTX_LEN = 256
GRID_W = 64
ROPE_THETA = 10000.0
EPS = 1e-6

MLA_HEADS = 4
MLA_Q_RANK = 384
MLA_KV_RANK = 256
MLA_NOPE = 64
MLA_ROPE = 32
MLA_V = 64
NA_HEADS = 4
NA_DIM = 64
NA_ROWS = 8
NA_COLS = 16
DIFF_HEADS = 4
DIFF_QK = 32
DIFF_V = 64
GQA_HEADS = 4
GQA_KV_HEADS = 2
GQA_DIM = 64
N_BRANCH = 4
BRANCH_W = 256
D_FF = -(-8 * D_MODEL // (3 * 256)) * 256
DEEPNORM_ALPHA = (2 * DEPTH) ** 0.25

IN_SIZES = (
    MLA_Q_RANK, MLA_KV_RANK, MLA_ROPE,
    NA_HEADS * NA_DIM, NA_HEADS * NA_DIM, NA_HEADS * NA_DIM,
    DIFF_HEADS * 2 * DIFF_QK, DIFF_HEADS * 2 * DIFF_QK, DIFF_HEADS * DIFF_V,
    GQA_HEADS * GQA_DIM, GQA_KV_HEADS * GQA_DIM, GQA_KV_HEADS * GQA_DIM,
    N_BRANCH * D_MODEL,
)
IN_OFFS = tuple(int(v) for v in np.concatenate([[0], np.cumsum(IN_SIZES)]))
GATE_OFF = IN_OFFS[12]

LANES = 128
HEAD_PAD = LANES
V_PAD = 2 * LANES
TOK_TILE = 256
Q_TILE = 1024
STREAM_ROWS = 256
ATTN_TILE = 512
NA_GROUP_ROWS = 4
NA_GROUPS_PER_STEP = 4
NA_WIN_ROWS = 12
SCORE_COLS = 256
MAX_CHAINS = 8
KEY_TILE = 256
KEY_CHUNKS = 4
V_DIM = 64
VT_ROWS = 80
MASK_VALUE = -1e30
VMEM_LIMIT = 56 * 1024 * 1024

LOG2E = math.log2(math.e)
MLA_SCALE = (MLA_NOPE + MLA_ROPE) ** -0.5 * LOG2E
NA_SCALE = NA_DIM ** -0.5 * LOG2E
DIFF_SCALE = DIFF_QK ** -0.5 * LOG2E
GQA_SCALE = GQA_DIM ** -0.5 * LOG2E

P_CQ = 0
P_CKV = P_CQ + MLA_Q_RANK
P_KPE = P_CKV + MLA_KV_RANK
P_QN = P_KPE + HEAD_PAD
P_KN = P_QN + NA_HEADS * HEAD_PAD
P_VN = P_KN + NA_HEADS * HEAD_PAD
P_QD = P_VN + NA_HEADS * HEAD_PAD
P_KD = P_QD + DIFF_HEADS * HEAD_PAD
P_VD = P_KD + DIFF_HEADS * HEAD_PAD
P_QG = P_VD + DIFF_HEADS * HEAD_PAD
P_KG = P_QG + GQA_HEADS * HEAD_PAD
P_VG = P_KG + GQA_KV_HEADS * HEAD_PAD
P_END = P_VG + GQA_KV_HEADS * HEAD_PAD

NT_DIMS = (((1,), (1,)), ((), ()))


def _cparams(sem):
    return pltpu.CompilerParams(dimension_semantics=sem, vmem_limit_bytes=VMEM_LIMIT)


def _const_spec(shape):
    nd = len(shape)
    return pl.BlockSpec(shape, lambda *_: (0,) * nd, pipeline_mode=pl.Buffered(1))


def _layer_spec(w, layer):
    nd = w.ndim - 1
    return pl.BlockSpec((1,) + w.shape[1:], lambda *_: (layer,) + (0,) * nd, pipeline_mode=pl.Buffered(1))


def _pad_heads(w, n_heads, width, dup=False):
    lead = w.shape[:-1]
    h = w.reshape(lead + (n_heads, width))
    fill = h if dup else jnp.zeros(lead + (n_heads, HEAD_PAD - width), w.dtype)
    return jnp.concatenate([h, fill], axis=-1).reshape(lead + (n_heads * HEAD_PAD,))


def _small_weights(w_in):
    o = IN_OFFS
    seg = lambda i: w_in[..., o[i]:o[i + 1]]
    zeros = lambda n: jnp.zeros(w_in.shape[:-1] + (n,), w_in.dtype)
    parts = [seg(0), seg(1), zeros(MLA_NOPE), seg(2), zeros(HEAD_PAD - MLA_NOPE - MLA_ROPE),
             _pad_heads(seg(3), NA_HEADS, 64), _pad_heads(seg(4), NA_HEADS, 64), _pad_heads(seg(5), NA_HEADS, 64, True),
             _pad_heads(seg(6), DIFF_HEADS, 64), _pad_heads(seg(7), DIFF_HEADS, 64), _pad_heads(seg(8), DIFF_HEADS, 64, True),
             _pad_heads(seg(9), GQA_HEADS, 64), _pad_heads(seg(10), GQA_KV_HEADS, 64),
             _pad_heads(seg(11), GQA_KV_HEADS, 64, True)]
    out = jnp.concatenate(parts, axis=-1)
    assert out.shape[-1] == P_END
    return out.astype(BF16)


def _kv_up_weights(w_kv_up):
    lead = w_kv_up.shape[:-1]
    h = w_kv_up.reshape(lead + (MLA_HEADS, MLA_NOPE + MLA_V))
    k = _pad_heads(h[..., :MLA_NOPE].reshape(lead + (-1,)), MLA_HEADS, MLA_NOPE)
    v = _pad_heads(h[..., MLA_NOPE:].reshape(lead + (-1,)), MLA_HEADS, MLA_V, True)
    return k.astype(BF16), v.astype(BF16)


def _rope_pattern(S, rot_dim):
    t = jnp.arange(S)
    pos = jnp.stack([t // GRID_W, t % GRID_W], axis=-1).astype(F32)
    n_f = rot_dim // 4
    inv = ROPE_THETA ** (-jnp.arange(n_f, dtype=F32) / n_f)
    ang = pos[:, :, None] * inv
    cos, sin = jnp.cos(ang), jnp.sin(ang)
    c = jnp.concatenate([cos[:, 0], cos[:, 0], cos[:, 1], cos[:, 1]], axis=-1)
    s = jnp.concatenate([-sin[:, 0], sin[:, 0], -sin[:, 1], sin[:, 1]], axis=-1)
    c = jnp.concatenate([c, jnp.ones((CTX_LEN, rot_dim), F32)], axis=0)
    s = jnp.concatenate([s, jnp.zeros((CTX_LEN, rot_dim), F32)], axis=0)
    return c, s


def _rope_tables(S):
    T = S + CTX_LEN
    c32, s32 = _rope_pattern(S, 32)
    c64, s64 = _rope_pattern(S, 64)
    one = lambda n: jnp.ones((T, n), F32)
    zero = lambda n: jnp.zeros((T, n), F32)
    cat = lambda xs: jnp.concatenate(xs, axis=-1)
    ca, sa = cat([one(64), c32, one(32)]), cat([zero(64), s32, zero(32)])
    cd, sd = cat([c32, c32, one(64)]), cat([s32, s32, zero(64)])
    cg, sg = cat([c64, one(64)]), cat([s64, zero(64)])
    return ca, sa, cd, sd, cg, sg


def _na_bias_tables(rpb, rows):
    depth, n_heads = rpb.shape[:2]
    n_groups = rows // NA_GROUP_ROWS
    kh, kw = min(NA_ROWS, rows), NA_COLS
    w = np.arange(GRID_W)[:, None]
    c = np.arange(GRID_W)[None, :]
    cs = np.clip(w - kw // 2, 0, GRID_W - kw)
    col_valid = (c >= cs) & (c < cs + kw)
    col_off = c - w + (NA_COLS - 1)
    onehot_c = ((col_off[None] == np.arange(2 * NA_COLS - 1)[:, None, None]) & col_valid[None]).astype(np.float32)
    a = np.arange(NA_GROUP_ROWS)[:, None]
    i = np.arange(NA_WIN_ROWS)[None, :]
    out = []
    for g in (0, 1, n_groups - 1):
        r = NA_GROUP_ROWS * g + a
        i_abs = np.clip(NA_GROUP_ROWS * g - kh // 2, 0, rows - NA_WIN_ROWS) + i
        rs = np.clip(r - kh // 2, 0, rows - kh)
        row_valid = (i_abs >= rs) & (i_abs < rs + kh)
        row_off = i_abs - r + (NA_ROWS - 1)
        onehot_r = ((row_off[..., None] == np.arange(2 * NA_ROWS - 1)) & row_valid[..., None]).astype(np.float32)
        valid = row_valid[:, None, :, None] & col_valid[None, :, None, :]
        b = jnp.einsum('air,lhrk,kwc->lhawic', jnp.asarray(onehot_r), rpb.astype(F32), jnp.asarray(onehot_c),
                       precision=lax.Precision.HIGHEST)
        b = jnp.where(jnp.asarray(valid)[None, None], b * LOG2E, MASK_VALUE)
        out.append(b.reshape(depth, n_heads, NA_GROUP_ROWS * GRID_W, NA_WIN_ROWS * GRID_W))
    return jnp.stack(out, axis=1).reshape((depth * 3,) + out[0].shape[1:])


def _sigmoid(x):
    return 1.0 / (1.0 + jnp.exp(-x))


def _rms(x, g, n):
    ms = jnp.sum(x * x, axis=-1, keepdims=True) * (1.0 / n)
    return x * lax.rsqrt(ms + EPS) * g


def _layer_norm(x, g, b):
    mu = jnp.mean(x, axis=-1, keepdims=True)
    xc = x - mu
    var = jnp.mean(xc * xc, axis=-1, keepdims=True)
    return xc * lax.rsqrt(var + EPS) * g + b


def _rope(z, c, s, n_f):
    lane = lax.broadcasted_iota(jnp.int32, z.shape, 1)
    first = (lane & n_f) == 0
    up = pltpu.roll(z, LANES - n_f, 1)
    dn = pltpu.roll(z, n_f, 1)
    return z * c + jnp.where(first, up, dn) * s


def _bdot(a, b):
    return jnp.dot(a, b, preferred_element_type=F32)


def _row_streams(n_rows):
    step = min(STREAM_ROWS, n_rows)
    return [slice(r, r + step) for r in range(0, n_rows, step)]


def _ada_kernel(c_ref, w_ref, b_ref, o_ref):
    c = c_ref[...]
    a = c * _sigmoid(c)
    o_ref[0] = jnp.dot(a, w_ref[0], preferred_element_type=F32, precision=lax.Precision.HIGHEST) + b_ref[0]


def _ada_call(c8, w_ada, b_ada):
    L, D, N = w_ada.shape
    tn = 1536
    return pl.pallas_call(
        _ada_kernel,
        grid=(L, N // tn),
        in_specs=[
            pl.BlockSpec((8, D), lambda l, n: (0, 0)),
            pl.BlockSpec((1, D, tn), lambda l, n: (l, 0, n)),
            pl.BlockSpec((1, 1, tn), lambda l, n: (l, 0, n)),
        ],
        out_specs=pl.BlockSpec((1, 8, tn), lambda l, n: (l, 0, n)),
        out_shape=jax.ShapeDtypeStruct((L, 8, N), F32),
        compiler_params=_cparams(("parallel", "parallel")),
        name="ada_mod",
    )(c8, w_ada, b_ada.reshape(L, 1, N))


def _proj_kernel(x_ref, ctx_ref, modl_ref, modc_ref, w_ref, gqa_ref, wqup_ref, gkva_ref, wkvk_ref, wkvv_ref,
                 gqn_ref, gkn_ref, ca_ref, sa_ref, cd_ref, sd_ref, cg_ref, sg_ref,
                 qa_ref, ka_ref, va_ref, qn_ref, kn_ref, vn_ref, qd_ref, kd_ref, vd_ref, qg_ref, kg_ref, vg_ref,
                 *, n_lat_tiles):
    D = D_MODEL
    is_ctx = pl.program_id(0) == n_lat_tiles
    x = jnp.where(is_ctx, ctx_ref[0], x_ref[0])
    ml, mc = modl_ref[0], modc_ref[0]
    sh = jnp.where(is_ctx, mc[:, 0:D], ml[:, 0:D])
    sc = jnp.where(is_ctx, mc[:, D:2 * D], ml[:, D:2 * D])
    h = (x * (1.0 + sc) + sh).astype(BF16)
    z = _bdot(h, w_ref[0])
    tm = z.shape[0]
    ones = jnp.ones((tm, LANES), F32)
    lane = lax.broadcasted_iota(jnp.int32, (tm, LANES), 1)
    head = lambda base, i: z[:, base + HEAD_PAD * i: base + HEAD_PAD * (i + 1)]
    with_ones = lambda v: jnp.concatenate([v, ones], axis=-1).astype(BF16)
    tail = (lax.broadcasted_iota(jnp.int32, (VT_ROWS - V_DIM, tm), 0) == 0).astype(F32)
    transposed = lambda v: jnp.concatenate([v.T[:V_DIM], tail], axis=0).astype(BF16)

    ca, sa = ca_ref[...], sa_ref[...]
    cqn = _rms(z[:, P_CQ:P_CKV], gqa_ref[...], MLA_Q_RANK).astype(BF16)
    qa_full = _bdot(cqn, wqup_ref[0])
    ckvn = _rms(z[:, P_CKV:P_KPE], gkva_ref[...], MLA_KV_RANK).astype(BF16)
    k_full = _bdot(ckvn, wkvk_ref[0])
    v_full = _bdot(ckvn, wkvv_ref[0])
    kpe = _rope(z[:, P_KPE:P_QN], ca, sa, MLA_ROPE // 4)
    for i in range(MLA_HEADS):
        sl = slice(HEAD_PAD * i, HEAD_PAD * (i + 1))
        qa_ref[0, i] = (_rope(qa_full[:, sl], ca, sa, MLA_ROPE // 4) * MLA_SCALE).astype(BF16)
        ka_ref[0, i] = (k_full[:, sl] + kpe).astype(BF16)
        va_ref[0, i] = transposed(v_full[:, sl])

    for i in range(NA_HEADS):
        qn_ref[0, i] = (head(P_QN, i) * NA_SCALE).astype(BF16)
        kn_ref[0, i] = head(P_KN, i).astype(BF16)
        vn_ref[0, i] = with_ones(head(P_VN, i))

    cd, sd = cd_ref[...], sd_ref[...]
    for i in range(DIFF_HEADS):
        zq = _rope(head(P_QD, i), cd, sd, DIFF_QK // 4) * DIFF_SCALE
        qd_ref[0, 2 * i] = jnp.where(lane < DIFF_QK, zq, 0.0).astype(BF16)
        qd_ref[0, 2 * i + 1] = jnp.where(lane >= DIFF_QK, zq, 0.0).astype(BF16)
        kd_ref[0, i] = _rope(head(P_KD, i), cd, sd, DIFF_QK // 4).astype(BF16)
        vd_ref[0, i] = transposed(head(P_VD, i))

    cg, sg = cg_ref[...], sg_ref[...]
    gqn, gkn = gqn_ref[...], gkn_ref[...]
    for i in range(GQA_HEADS):
        qn = _rms(head(P_QG, i), gqn, GQA_DIM)
        qg_ref[0, i] = (_rope(qn, cg, sg, GQA_DIM // 4) * GQA_SCALE).astype(BF16)
    for i in range(GQA_KV_HEADS):
        kn = _rms(head(P_KG, i), gkn, GQA_DIM)
        kg_ref[0, i] = _rope(kn, cg, sg, GQA_DIM // 4).astype(BF16)
        vg_ref[0, i] = transposed(head(P_VG, i))


def _proj_call(xl, xc, mod_l, mod_c, wsm, gqa, wqup, gkva, wkvk, wkvv, gqn, gkn, tabs, layer):
    B, S, D = xl.shape
    tm = TOK_TILE
    n_lat = S // tm
    T = S + CTX_LEN
    grid = (n_lat + 1, B)
    tab_spec = pl.BlockSpec((tm, LANES), lambda t, b: (t, 0))
    in_specs = [
        pl.BlockSpec((1, tm, D), lambda t, b: (b, jnp.minimum(t, n_lat - 1), 0)),
        pl.BlockSpec((1, tm, D), lambda t, b: (b, 0, 0)),
        pl.BlockSpec((1, 1, 6 * D), lambda t, b: (b, 0, 0)),
        pl.BlockSpec((1, 1, 6 * D), lambda t, b: (0, 0, 0)),
        _layer_spec(wsm, layer), _const_spec(gqa.shape), _layer_spec(wqup, layer), _const_spec(gkva.shape),
        _layer_spec(wkvk, layer), _layer_spec(wkvv, layer), _const_spec(gqn.shape), _const_spec(gkn.shape),
    ] + [tab_spec] * 6

    def o(n_heads, width):
        return (jax.ShapeDtypeStruct((B, n_heads, T, width), BF16),
                pl.BlockSpec((1, n_heads, tm, width), lambda t, b: (b, 0, t, 0)))

    def ot(n_heads):
        return (jax.ShapeDtypeStruct((B, n_heads, VT_ROWS, T), BF16),
                pl.BlockSpec((1, n_heads, VT_ROWS, tm), lambda t, b: (b, 0, 0, t)))

    outs = [o(MLA_HEADS, HEAD_PAD), o(MLA_HEADS, HEAD_PAD), ot(MLA_HEADS),
            o(NA_HEADS, HEAD_PAD), o(NA_HEADS, HEAD_PAD), o(NA_HEADS, V_PAD),
            o(2 * DIFF_HEADS, HEAD_PAD), o(DIFF_HEADS, HEAD_PAD), ot(DIFF_HEADS),
            o(GQA_HEADS, HEAD_PAD), o(GQA_KV_HEADS, HEAD_PAD), ot(GQA_KV_HEADS)]
    return pl.pallas_call(
        functools.partial(_proj_kernel, n_lat_tiles=n_lat),
        grid=grid,
        in_specs=in_specs,
        out_specs=[s for _, s in outs],
        out_shape=[s for s, _ in outs],
        compiler_params=_cparams(("arbitrary", "arbitrary")),
        name="in_proj",
    )(xl, xc, mod_l, mod_c, wsm, gqa, wqup, gkva, wkvk, wkvv, gqn, gkn, *tabs)


def _key_chunks(t_keys):
    n_tiles = t_keys // KEY_TILE
    n_chunks = min(KEY_CHUNKS, n_tiles)
    sizes = [n_tiles // n_chunks + (1 if c < n_tiles % n_chunks else 0) for c in range(n_chunks)]
    bounds = np.cumsum([0] + sizes) * KEY_TILE
    return [(int(bounds[c]), int(bounds[c + 1])) for c in range(n_chunks)]


def _attn_kernel(*refs, n_kv, group, diff, lam_init):
    q_ref, k_ref, v_ref = refs[:3]
    s_ref, ma_ref, mb_ref, pa_ref, pb_ref = refs[-5:]
    step = pl.program_id(0)
    tq = q_ref.shape[2]
    n_dots = s_ref.shape[0]
    cols = s_ref.shape[2]
    split = n_dots // n_kv

    @pl.when(step == 0)
    def _():
        s_ref[...] = jnp.zeros(s_ref.shape, F32)
        mb_ref[...] = jnp.zeros(mb_ref.shape, F32)
        pb_ref[...] = jnp.ones(pb_ref.shape, BF16)

    def body(m_write, m_read, p_write, p_read):
        qs = []
        for j in range(n_kv):
            q = q_ref[0, j * group:(j + 1) * group].reshape(group * tq, HEAD_PAD)
            qs += [q[r * cols:(r + 1) * cols] for r in range(split)]
        acc = [None] * n_dots
        run_max = [None] * n_dots
        for lo, hi in _key_chunks(k_ref.shape[2]):
            for d in range(n_dots):
                j = d // split
                part = _bdot(v_ref[0, j, :, lo:hi], p_read[d, lo:hi, :])
                acc[d] = part if acc[d] is None else acc[d] + part
                p_write[d, lo:hi, :] = jnp.exp2(s_ref[d, lo:hi, :] - m_read[d]).astype(BF16)
                s = lax.dot_general(k_ref[0, j, lo:hi, :], qs[d], NT_DIMS, preferred_element_type=F32)
                s_ref[d, lo:hi, :] = s
                pm = jnp.max(s.reshape(MAX_CHAINS, (hi - lo) // MAX_CHAINS, cols), axis=1)
                run_max[d] = pm if run_max[d] is None else jnp.maximum(run_max[d], pm)
        heads = []
        for j in range(n_kv):
            outs = []
            for d in range(j * split, (j + 1) * split):
                m_write[d] = jnp.max(run_max[d], axis=0, keepdims=True)
                outs.append(acc[d][:V_DIM] / acc[d][V_DIM:V_DIM + 1])
            on = outs[0] if split == 1 else jnp.concatenate(outs, axis=1)
            heads += [on[:, g * tq:(g + 1) * tq] for g in range(group)]
        _attn_finish(heads, refs, diff, lam_init)

    parity = lax.rem(step, 2)

    @pl.when(parity == 0)
    def _():
        body(ma_ref, mb_ref, pa_ref, pb_ref)

    @pl.when(parity == 1)
    def _():
        body(mb_ref, ma_ref, pb_ref, pa_ref)


def _attn_finish(heads, refs, diff, lam_init):
    if diff:
        lq1_ref, lk1_ref, lq2_ref, lk2_ref, gsub_ref, o_ref = refs[3:9]
    else:
        o_ref = refs[3]
    if diff:
        ys = []
        for j in range(2):
            dot_l = lambda a, b: jnp.sum(a[0, j:j + 1] * b[0, j:j + 1], axis=-1, keepdims=True)
            lam = jnp.exp(dot_l(lq1_ref, lk1_ref)) - jnp.exp(dot_l(lq2_ref, lk2_ref)) + lam_init
            y = heads[2 * j] - lam * heads[2 * j + 1]
            ms = jnp.mean(y * y, axis=0, keepdims=True)
            ys.append(y * lax.rsqrt(ms + EPS) * gsub_ref[...] * (1.0 - lam_init))
        heads = ys
    o_ref[0] = jnp.concatenate(heads, axis=0).T.astype(BF16)


def _transpose_v(v):
    B, H, T, _ = v.shape
    vt = jnp.swapaxes(v[..., :V_DIM], -1, -2)
    tail = jnp.zeros((B, H, VT_ROWS - V_DIM, T), v.dtype).at[:, :, 0, :].set(1.0)
    return jnp.concatenate([vt, tail], axis=2)


def _attn_call(q, k, vt, *, group, n_kv, tq, q_blk0, n_q, t_blk, k_blk0, diff_args=None, lam_init=0.0, name):
    B, Hq = q.shape[0], q.shape[1]
    n_pairs = Hq // (n_kv * group)
    qh = n_kv * group
    n_units = B * n_pairs * n_q
    n_dots = qh * tq // SCORE_COLS

    def unit(u):
        return u // (n_pairs * n_q), (u // n_q) % n_pairs, u % n_q

    def stage1(s):
        return unit(jnp.minimum(s, n_units - 1))

    def stage3(s):
        return unit(jnp.maximum(s - 2, 0))

    def q_map(s):
        b, p, i = stage1(s)
        return b, p, q_blk0 + i, 0

    def k_map(s):
        b, p, _ = stage1(s)
        return b, p, k_blk0, 0

    def v_map(s):
        b, p, _ = stage3(s)
        return b, p, 0, k_blk0

    def o_map(s):
        b, p, i = stage3(s)
        return b, i, p

    in_specs = [
        pl.BlockSpec((1, qh, tq, HEAD_PAD), q_map),
        pl.BlockSpec((1, n_kv, t_blk, HEAD_PAD), k_map),
        pl.BlockSpec((1, n_kv, VT_ROWS, t_blk), v_map),
    ]
    args = [q, k, vt]
    if diff_args is not None:
        lam_spec = pl.BlockSpec((1, 2, DIFF_QK), lambda s: (stage3(s)[1], 0, 0))
        in_specs += [lam_spec] * 4 + [pl.BlockSpec((V_DIM, 1), lambda s: (0, 0))]
        args += list(diff_args)
    score_buf = pltpu.VMEM((n_dots, t_blk, SCORE_COLS), F32)
    prob_buf = pltpu.VMEM((n_dots, t_blk, SCORE_COLS), BF16)
    max_buf = pltpu.VMEM((n_dots, 1, SCORE_COLS), F32)
    return pl.pallas_call(
        functools.partial(_attn_kernel, n_kv=n_kv, group=group, diff=diff_args is not None, lam_init=lam_init),
        grid=(n_units + 2,),
        in_specs=in_specs,
        out_specs=pl.BlockSpec((1, tq, LANES), o_map),
        out_shape=jax.ShapeDtypeStruct((B, n_q * tq, n_pairs * LANES), BF16),
        scratch_shapes=[score_buf, max_buf, max_buf, prob_buf, prob_buf],
        compiler_params=_cparams(("arbitrary",)),
        name=name,
    )(*args)


def _na_kernel(q_ref, k_ref, v_ref, *rest, n_lat, win):
    bias_refs, o_ref = rest[:-1], rest[-1]
    rows = n_lat // GRID_W
    tq = NA_GROUP_ROWS * GRID_W
    lane = lax.broadcasted_iota(jnp.int32, (tq, LANES), 1)
    for i, bias_ref in enumerate(bias_refs):
        g = pl.program_id(1) * len(bias_refs) + i
        start = jnp.clip(NA_GROUP_ROWS * g - NA_ROWS // 2, 0, rows - NA_WIN_ROWS) * GRID_W
        start = pl.multiple_of(start, GRID_W)
        outs = []
        for h in range(NA_HEADS):
            q = q_ref[0, h, i * tq:(i + 1) * tq, :]
            s_l = lax.dot_general(q, k_ref[0, h, pl.ds(start, win), :], NT_DIMS, preferred_element_type=F32)
            s_l = s_l + bias_ref[0, h]
            s_c = lax.dot_general(q, k_ref[0, h, n_lat:, :], NT_DIMS, preferred_element_type=F32)
            m = jnp.maximum(jnp.max(s_l, axis=-1, keepdims=True), jnp.max(s_c, axis=-1, keepdims=True))
            p_l = jnp.exp2(s_l - m).astype(BF16)
            p_c = jnp.exp2(s_c - m).astype(BF16)
            oe = _bdot(p_l, v_ref[0, h, pl.ds(start, win), :]) + _bdot(p_c, v_ref[0, h, n_lat:, :])
            outs.append(oe[:, :LANES] / oe[:, LANES:])
        out = jnp.concatenate([jnp.where(lane < 64, outs[0], outs[1]), jnp.where(lane < 64, outs[2], outs[3])], axis=-1)
        o_ref[0, i * tq:(i + 1) * tq, :] = out.astype(BF16)


def _na_call(q, k, v, bias, n_lat, layer):
    B, H, T, _ = q.shape
    tq = NA_GROUP_ROWS * GRID_W
    win = NA_WIN_ROWS * GRID_W
    n_groups = n_lat // tq
    per_step = NA_GROUPS_PER_STEP

    def bias_spec(i):
        def variant(b, s):
            g = s * per_step + i
            return 3 * layer + jnp.where(g == 0, 0, jnp.where(g == n_groups - 1, 2, 1)), 0, 0, 0
        return pl.BlockSpec((1, H, tq, win), variant)

    return pl.pallas_call(
        functools.partial(_na_kernel, n_lat=n_lat, win=win),
        grid=(B, n_groups // per_step),
        in_specs=[
            pl.BlockSpec((1, H, per_step * tq, HEAD_PAD), lambda b, s: (b, 0, s, 0)),
            pl.BlockSpec((1, H, T, HEAD_PAD), lambda b, s: (b, 0, 0, 0)),
            pl.BlockSpec((1, H, T, V_PAD), lambda b, s: (b, 0, 0, 0)),
        ] + [bias_spec(i) for i in range(per_step)],
        out_specs=pl.BlockSpec((1, per_step * tq, H * 64), lambda b, s: (b, s, 0)),
        out_shape=jax.ShapeDtypeStruct((B, n_lat, H * 64), BF16),
        compiler_params=_cparams(("parallel", "arbitrary")),
        name="na_attn",
    )(q, k, v, *([bias] * per_step))


def _merge_kernel(x_ref, mod_ref, ya_ref, yb_ref, yc_ref, yd_ref, wg_ref, wb_ref, wo_ref, lg_ref, lb_ref, o_ref):
    D = D_MODEL
    mod = mod_ref[0]
    sh, sc, gate_res = mod[:, 0:D], mod[:, D:2 * D], mod[:, 2 * D:3 * D]
    for rows in _row_streams(x_ref.shape[1]):
        x = x_ref[0, rows]
        h = (x * (1.0 + sc) + sh).astype(BF16)
        acc = None
        for i, y_ref in enumerate((ya_ref, yb_ref, yc_ref, yd_ref)):
            gate = _sigmoid(_bdot(h, wg_ref[0, :, i * D:(i + 1) * D]))
            term = gate * _bdot(y_ref[0, rows], wb_ref[0, i])
            acc = term if acc is None else acc + term
        out = _bdot(acc.astype(BF16), wo_ref[0])
        o_ref[0, rows] = _layer_norm(DEEPNORM_ALPHA * x + gate_res * out, lg_ref[...], lb_ref[...])


def _merge_call(x, mod, ys, wg, wb, wo, lg, lb, tm, layer):
    B, S, D = x.shape
    per_batch_mod = mod.shape[0] > 1
    tok = lambda w: pl.BlockSpec((1, tm, w), lambda b, i: (b, i, 0))
    return pl.pallas_call(
        _merge_kernel,
        grid=(B, S // tm),
        in_specs=[tok(D), pl.BlockSpec((1, 1, 6 * D), lambda b, i: (b if per_batch_mod else 0, 0, 0))]
        + [tok(BRANCH_W)] * 4
        + [_layer_spec(wg, layer), _layer_spec(wb, layer), _layer_spec(wo, layer), _const_spec(lg.shape), _const_spec(lb.shape)],
        out_specs=tok(D),
        out_shape=jax.ShapeDtypeStruct((B, S, D), F32),
        compiler_params=_cparams(("parallel", "arbitrary")),
        name="merge_ln",
    )(x, mod, *ys, wg, wb, wo, lg, lb)


def _ffn_kernel(x_ref, mod_ref, wgu_ref, wd_ref, lg_ref, lb_ref, o_ref):
    D = D_MODEL
    mod = mod_ref[0]
    sh, sc, gate_res = mod[:, 3 * D:4 * D], mod[:, 4 * D:5 * D], mod[:, 5 * D:6 * D]
    for rows in _row_streams(x_ref.shape[1]):
        x = x_ref[0, rows]
        h = (x * (1.0 + sc) + sh).astype(BF16)
        gu = _bdot(h, wgu_ref[0])
        g, u = gu[:, :D_FF], gu[:, D_FF:]
        act = (g * _sigmoid(g) * u).astype(BF16)
        down = _bdot(act, wd_ref[0])
        o_ref[0, rows] = _layer_norm(DEEPNORM_ALPHA * x + gate_res * down, lg_ref[...], lb_ref[...])


def _ffn_call(x, mod, wgu, wd, lg, lb, tm, layer):
    B, S, D = x.shape
    per_batch_mod = mod.shape[0] > 1
    tok = pl.BlockSpec((1, tm, D), lambda b, i: (b, i, 0))
    return pl.pallas_call(
        _ffn_kernel,
        grid=(B, S // tm),
        in_specs=[tok, pl.BlockSpec((1, 1, 6 * D), lambda b, i: (b if per_batch_mod else 0, 0, 0)),
                  _layer_spec(wgu, layer), _layer_spec(wd, layer), _const_spec(lg.shape), _const_spec(lb.shape)],
        out_specs=tok,
        out_shape=jax.ShapeDtypeStruct((B, S, D), F32),
        compiler_params=_cparams(("parallel", "arbitrary")),
        name="ffn_ln",
    )(x, mod, wgu, wd, lg, lb)


def kernel(x, c, ctx, c_ctx, w_ada, b_ada, w_in, g_q_a, w_q_up, g_kv_a, w_kv_up, rpb, lam_q1, lam_k1, lam_q2, lam_k2,
           g_sub, g_qn, g_kn, w_branch, w_out, ln1_g, ln1_b, w_gate_up, w_down, ln2_g, ln2_b):
    B, S, D = x.shape
    rows = S // GRID_W
    n_q = S // Q_TILE
    T = S + CTX_LEN
    ctx_blk = S // CTX_LEN

    c8 = jnp.concatenate([c, c_ctx[None, :], jnp.zeros((8 - B - 1, D), F32)], axis=0)
    mod = _ada_call(c8, w_ada, b_ada)
    tabs = _rope_tables(S)
    w_in_bf = w_in.astype(BF16)
    wsm = _small_weights(w_in_bf)
    wg = w_in_bf[..., GATE_OFF:]
    wqup = _pad_heads(w_q_up, MLA_HEADS, MLA_NOPE + MLA_ROPE).astype(BF16)
    wkvk, wkvv = _kv_up_weights(w_kv_up)
    wb, wo = w_branch.astype(BF16), w_out.astype(BF16)
    wgu, wd = w_gate_up.astype(BF16), w_down.astype(BF16)
    bias = _na_bias_tables(rpb, rows)
    row = lambda v: v.reshape(1, -1)
    pad64 = lambda v: jnp.concatenate([v, jnp.zeros_like(v)]).reshape(1, LANES)

    xl, xc = x, ctx
    for l in range(DEPTH):
        need_ctx = l < DEPTH - 1
        lam_init = 0.8 - 0.6 * math.exp(-0.3 * l)
        mod_l = mod[l, :B].reshape(B, 1, 6 * D)
        mod_c = mod[l, B:B + 1].reshape(1, 1, 6 * D)
        (qa, ka, vat, qn, kn, vn, qd, kd, vdt, qg, kg, vgt) = _proj_call(
            xl, xc, mod_l, mod_c, wsm, row(g_q_a[l]), wqup, row(g_kv_a[l]), wkvk, wkvv,
            pad64(g_qn[l]), pad64(g_kn[l]), tabs, layer=l)

        diff_args = (lam_q1[l].reshape(2, 2, DIFF_QK), lam_k1[l].reshape(2, 2, DIFF_QK),
                     lam_q2[l].reshape(2, 2, DIFF_QK), lam_k2[l].reshape(2, 2, DIFF_QK),
                     g_sub[l].reshape(V_DIM, 1))

        lat = lambda tq: dict(tq=tq, q_blk0=0, n_q=S // tq, t_blk=T, k_blk0=0)
        ya = _attn_call(qa, ka, vat, group=1, n_kv=2, name="mla_attn", **lat(ATTN_TILE))
        yb = _na_call(qn, kn, vn, bias, S, layer=l)
        yc = _attn_call(qd, kd, vdt, group=2, n_kv=2, diff_args=diff_args, lam_init=lam_init, name="diff_attn",
                        **lat(ATTN_TILE // 2))
        yd = _attn_call(qg, kg, vgt, group=2, n_kv=1, name="gqa_attn", **lat(ATTN_TILE))

        ln1 = (row(ln1_g[l]), row(ln1_b[l]))
        ln2 = (row(ln2_g[l]), row(ln2_b[l]))

        if need_ctx:
            cx = dict(tq=CTX_LEN, q_blk0=ctx_blk, n_q=1, t_blk=CTX_LEN, k_blk0=ctx_blk)
            ya_c = _attn_call(qa, ka, vat, group=1, n_kv=2, name="mla_attn_ctx", **cx)
            ctx0 = dict(tq=CTX_LEN, q_blk0=0, n_q=1, t_blk=CTX_LEN, k_blk0=0)
            yb_c = _attn_call(qn[:, :, S:], kn[:, :, S:], _transpose_v(vn[:, :, S:]), group=1, n_kv=2,
                              name="na_attn_ctx", **ctx0)
            yc_c = _attn_call(qd, kd, vdt, group=2, n_kv=2, diff_args=diff_args, lam_init=lam_init,
                              name="diff_attn_ctx", **cx)
            yd_c = _attn_call(qg, kg, vgt, group=2, n_kv=1, name="gqa_attn_ctx", **cx)
            xc1 = _merge_call(xc, mod_c, (ya_c, yb_c, yc_c, yd_c), wg, wb, wo, *ln1, tm=CTX_LEN, layer=l)
            xc = _ffn_call(xc1, mod_c, wgu, wd, *ln2, tm=CTX_LEN, layer=l)

        xl1 = _merge_call(xl, mod_l, (ya, yb, yc, yd), wg, wb, wo, *ln1, tm=Q_TILE, layer=l)
        xl = _ffn_call(xl1, mod_l, wgu, wd, *ln2, tm=Q_TILE, layer=l)
    return xl
```

```python
import functools
import math

import jax
import jax.numpy as jnp
import numpy as np
from jax import lax
from jax.experimental import pallas as pl
from jax.experimental.pallas import tpu as pltpu

F32 = jnp.float32
BF16 = jnp.bfloat16

D_MODEL = 1024
DEPTH = 2
CTX_LEN = 256
GRID_W = 64
ROPE_THETA = 10000.0
EPS = 1e-6

MLA_HEADS = 4
MLA_Q_RANK = 384
MLA_KV_RANK = 256
MLA_NOPE = 64
MLA_ROPE = 32
MLA_V = 64
NA_HEADS = 4
NA_DIM = 64
NA_ROWS = 8
NA_COLS = 16
DIFF_HEADS = 4
DIFF_QK = 32
DIFF_V = 64
GQA_HEADS = 4
GQA_KV_HEADS = 2
GQA_DIM = 64
N_BRANCH = 4
BRANCH_W = 256
D_FF = -(-8 * D_MODEL // (3 * 256)) * 256
DEEPNORM_ALPHA = (2 * DEPTH) ** 0.25

IN_SIZES = (
    MLA_Q_RANK, MLA_KV_RANK, MLA_ROPE,
    NA_HEADS * NA_DIM, NA_HEADS * NA_DIM, NA_HEADS * NA_DIM,
    DIFF_HEADS * 2 * DIFF_QK, DIFF_HEADS * 2 * DIFF_QK, DIFF_HEADS * DIFF_V,
    GQA_HEADS * GQA_DIM, GQA_KV_HEADS * GQA_DIM, GQA_KV_HEADS * GQA_DIM,
    N_BRANCH * D_MODEL,
)
IN_OFFS = tuple(int(v) for v in np.concatenate([[0], np.cumsum(IN_SIZES)]))
GATE_OFF = IN_OFFS[12]

LANES = 128
HEAD_PAD = LANES
V_PAD = 2 * LANES
TOK_TILE = 256
Q_TILE = 1024
FUSED_TILE = 512
STREAM_ROWS = 256
ATTN_TILE = 512
NA_GROUP_ROWS = 4
NA_GROUPS_PER_STEP = 4
NA_WIN_ROWS = 12
SCORE_COLS = 256
MAX_CHAINS = 8
KEY_TILE = 256
KEY_CHUNKS = 4
V_DIM = 64
VT_ROWS = 80
MASK_VALUE = -1e30
VMEM_LIMIT = 56 * 1024 * 1024

LOG2E = math.log2(math.e)
MLA_SCALE = (MLA_NOPE + MLA_ROPE) ** -0.5 * LOG2E
NA_SCALE = NA_DIM ** -0.5 * LOG2E
DIFF_SCALE = DIFF_QK ** -0.5 * LOG2E
GQA_SCALE = GQA_DIM ** -0.5 * LOG2E

P_CQ = 0
P_CKV = P_CQ + MLA_Q_RANK
P_KPE = P_CKV + MLA_KV_RANK
P_QN = P_KPE + HEAD_PAD
P_KN = P_QN + NA_HEADS * HEAD_PAD
P_VN = P_KN + NA_HEADS * HEAD_PAD
P_QD = P_VN + NA_HEADS * HEAD_PAD
P_KD = P_QD + DIFF_HEADS * HEAD_PAD
P_VD = P_KD + DIFF_HEADS * HEAD_PAD
P_QG = P_VD + DIFF_HEADS * HEAD_PAD
P_KG = P_QG + GQA_HEADS * HEAD_PAD
P_VG = P_KG + GQA_KV_HEADS * HEAD_PAD
P_END = P_VG + GQA_KV_HEADS * HEAD_PAD

NT_DIMS = (((1,), (1,)), ((), ()))


def _cparams(sem):
    return pltpu.CompilerParams(dimension_semantics=sem, vmem_limit_bytes=VMEM_LIMIT)


def _const_spec(shape):
    nd = len(shape)
    return pl.BlockSpec(shape, lambda *_: (0,) * nd, pipeline_mode=pl.Buffered(1))


def _layer_spec(w, layer):
    nd = w.ndim - 1
    return pl.BlockSpec((1,) + w.shape[1:], lambda *_: (layer,) + (0,) * nd, pipeline_mode=pl.Buffered(1))


def _pad_heads(w, n_heads, width, dup=False):
    lead = w.shape[:-1]
    h = w.reshape(lead + (n_heads, width))
    fill = h if dup else jnp.zeros(lead + (n_heads, HEAD_PAD - width), w.dtype)
    return jnp.concatenate([h, fill], axis=-1).reshape(lead + (n_heads * HEAD_PAD,))


def _small_weights(w_in):
    o = IN_OFFS
    seg = lambda i: w_in[..., o[i]:o[i + 1]]
    zeros = lambda n: jnp.zeros(w_in.shape[:-1] + (n,), w_in.dtype)
    parts = [seg(0), seg(1), zeros(MLA_NOPE), seg(2), zeros(HEAD_PAD - MLA_NOPE - MLA_ROPE),
             _pad_heads(seg(3), NA_HEADS, 64), _pad_heads(seg(4), NA_HEADS, 64), _pad_heads(seg(5), NA_HEADS, 64, True),
             _pad_heads(seg(6), DIFF_HEADS, 64), _pad_heads(seg(7), DIFF_HEADS, 64), _pad_heads(seg(8), DIFF_HEADS, 64, True),
             _pad_heads(seg(9), GQA_HEADS, 64), _pad_heads(seg(10), GQA_KV_HEADS, 64),
             _pad_heads(seg(11), GQA_KV_HEADS, 64, True)]
    out = jnp.concatenate(parts, axis=-1)
    assert out.shape[-1] == P_END
    return out.astype(BF16)


def _kv_up_weights(w_kv_up):
    lead = w_kv_up.shape[:-1]
    h = w_kv_up.reshape(lead + (MLA_HEADS, MLA_NOPE + MLA_V))
    k = _pad_heads(h[..., :MLA_NOPE].reshape(lead + (-1,)), MLA_HEADS, MLA_NOPE)
    v = _pad_heads(h[..., MLA_NOPE:].reshape(lead + (-1,)), MLA_HEADS, MLA_V, True)
    return k.astype(BF16), v.astype(BF16)


def _rope_pattern(S, rot_dim):
    t = jnp.arange(S)
    pos = jnp.stack([t // GRID_W, t % GRID_W], axis=-1).astype(F32)
    n_f = rot_dim // 4
    inv = ROPE_THETA ** (-jnp.arange(n_f, dtype=F32) / n_f)
    ang = pos[:, :, None] * inv
    cos, sin = jnp.cos(ang), jnp.sin(ang)
    c = jnp.concatenate([cos[:, 0], cos[:, 0], cos[:, 1], cos[:, 1]], axis=-1)
    s = jnp.concatenate([-sin[:, 0], sin[:, 0], -sin[:, 1], sin[:, 1]], axis=-1)
    c = jnp.concatenate([c, jnp.ones((CTX_LEN, rot_dim), F32)], axis=0)
    s = jnp.concatenate([s, jnp.zeros((CTX_LEN, rot_dim), F32)], axis=0)
    return c, s


def _rope_tables(S):
    T = S + CTX_LEN
    c32, s32 = _rope_pattern(S, 32)
    c64, s64 = _rope_pattern(S, 64)
    one = lambda n: jnp.ones((T, n), F32)
    zero = lambda n: jnp.zeros((T, n), F32)
    cat = lambda xs: jnp.concatenate(xs, axis=-1)
    ca, sa = cat([one(64), c32, one(32)]), cat([zero(64), s32, zero(32)])
    cd, sd = cat([c32, c32, one(64)]), cat([s32, s32, zero(64)])
    cg, sg = cat([c64, one(64)]), cat([s64, zero(64)])
    return ca, sa, cd, sd, cg, sg


def _na_bias_tables(rpb, rows):
    depth, n_heads = rpb.shape[:2]
    n_groups = rows // NA_GROUP_ROWS
    kh, kw = min(NA_ROWS, rows), NA_COLS
    w = np.arange(GRID_W)[:, None]
    c = np.arange(GRID_W)[None, :]
    cs = np.clip(w - kw // 2, 0, GRID_W - kw)
    col_valid = (c >= cs) & (c < cs + kw)
    col_off = c - w + (NA_COLS - 1)
    onehot_c = ((col_off[None] == np.arange(2 * NA_COLS - 1)[:, None, None]) & col_valid[None]).astype(np.float32)
    a = np.arange(NA_GROUP_ROWS)[:, None]
    i = np.arange(NA_WIN_ROWS)[None, :]
    out = []
    for g in (0, 1, n_groups - 1):
        r = NA_GROUP_ROWS * g + a
        i_abs = np.clip(NA_GROUP_ROWS * g - kh // 2, 0, rows - NA_WIN_ROWS) + i
        rs = np.clip(r - kh // 2, 0, rows - kh)
        row_valid = (i_abs >= rs) & (i_abs < rs + kh)
        row_off = i_abs - r + (NA_ROWS - 1)
        onehot_r = ((row_off[..., None] == np.arange(2 * NA_ROWS - 1)) & row_valid[..., None]).astype(np.float32)
        valid = row_valid[:, None, :, None] & col_valid[None, :, None, :]
        b = jnp.einsum('air,lhrk,kwc->lhawic', jnp.asarray(onehot_r), rpb.astype(F32), jnp.asarray(onehot_c),
                       precision=lax.Precision.HIGHEST)
        b = jnp.where(jnp.asarray(valid)[None, None], b * LOG2E, MASK_VALUE)
        out.append(b.reshape(depth, n_heads, NA_GROUP_ROWS * GRID_W, NA_WIN_ROWS * GRID_W))
    return jnp.stack(out, axis=1).reshape((depth * 3,) + out[0].shape[1:])


def _sigmoid(x):
    return 1.0 / (1.0 + jnp.exp(-x))


def _rms(x, g, n):
    ms = jnp.sum(x * x, axis=-1, keepdims=True) * (1.0 / n)
    return x * lax.rsqrt(ms + EPS) * g


def _layer_norm(x, g, b):
    mu = jnp.mean(x, axis=-1, keepdims=True)
    xc = x - mu
    var = jnp.mean(xc * xc, axis=-1, keepdims=True)
    return xc * lax.rsqrt(var + EPS) * g + b


def _rope(z, c, s, n_f):
    lane = lax.broadcasted_iota(jnp.int32, z.shape, 1)
    first = (lane & n_f) == 0
    up = pltpu.roll(z, LANES - n_f, 1)
    dn = pltpu.roll(z, n_f, 1)
    return z * c + jnp.where(first, up, dn) * s


def _bdot(a, b):
    return jnp.dot(a, b, preferred_element_type=F32)


def _row_streams(n_rows):
    step = min(STREAM_ROWS, n_rows)
    return [slice(r, r + step) for r in range(0, n_rows, step)]


def _ada_kernel(c_ref, w_ref, b_ref, o_ref):
    c = c_ref[...]
    a = c * _sigmoid(c)
    o_ref[0] = jnp.dot(a, w_ref[0], preferred_element_type=F32, precision=lax.Precision.HIGHEST) + b_ref[0]


def _ada_call(c8, w_ada, b_ada):
    L, D, N = w_ada.shape
    tn = 1536
    return pl.pallas_call(
        _ada_kernel,
        grid=(L, N // tn),
        in_specs=[
            pl.BlockSpec((8, D), lambda l, n: (0, 0)),
            pl.BlockSpec((1, D, tn), lambda l, n: (l, 0, n)),
            pl.BlockSpec((1, 1, tn), lambda l, n: (l, 0, n)),
        ],
        out_specs=pl.BlockSpec((1, 8, tn), lambda l, n: (l, 0, n)),
        out_shape=jax.ShapeDtypeStruct((L, 8, N), F32),
        compiler_params=_cparams(("parallel", "parallel")),
        name="ada_mod",
    )(c8, w_ada, b_ada.reshape(L, 1, N))


def _proj_kernel(x_ref, ctx_ref, modl_ref, modc_ref, w_ref, gqa_ref, wqup_ref, gkva_ref, wkvk_ref, wkvv_ref,
                 gqn_ref, gkn_ref, ca_ref, sa_ref, cd_ref, sd_ref, cg_ref, sg_ref,
                 qa_ref, ka_ref, va_ref, qn_ref, kn_ref, vn_ref, qd_ref, kd_ref, vd_ref, qg_ref, kg_ref, vg_ref,
                 *, n_lat_tiles):
    D = D_MODEL
    is_ctx = pl.program_id(0) == n_lat_tiles
    x = jnp.where(is_ctx, ctx_ref[0], x_ref[0])
    ml, mc = modl_ref[0], modc_ref[0]
    sh = jnp.where(is_ctx, mc[:, 0:D], ml[:, 0:D])
    sc = jnp.where(is_ctx, mc[:, D:2 * D], ml[:, D:2 * D])
    h = (x * (1.0 + sc) + sh).astype(BF16)
    z = _bdot(h, w_ref[0])
    tm = z.shape[0]
    ones = jnp.ones((tm, LANES), F32)
    lane = lax.broadcasted_iota(jnp.int32, (tm, LANES), 1)
    head = lambda base, i: z[:, base + HEAD_PAD * i: base + HEAD_PAD * (i + 1)]
    with_ones = lambda v: jnp.concatenate([v, ones], axis=-1).astype(BF16)
    tail = (lax.broadcasted_iota(jnp.int32, (VT_ROWS - V_DIM, tm), 0) == 0).astype(F32)
    transposed = lambda v: jnp.concatenate([v.T[:V_DIM], tail], axis=0).astype(BF16)

    ca, sa = ca_ref[...], sa_ref[...]
    cqn = _rms(z[:, P_CQ:P_CKV], gqa_ref[...], MLA_Q_RANK).astype(BF16)
    qa_full = _bdot(cqn, wqup_ref[0])
    ckvn = _rms(z[:, P_CKV:P_KPE], gkva_ref[...], MLA_KV_RANK).astype(BF16)
    k_full = _bdot(ckvn, wkvk_ref[0])
    v_full = _bdot(ckvn, wkvv_ref[0])
    kpe = _rope(z[:, P_KPE:P_QN], ca, sa, MLA_ROPE // 4)
    for i in range(MLA_HEADS):
        sl = slice(HEAD_PAD * i, HEAD_PAD * (i + 1))
        qa_ref[0, i] = (_rope(qa_full[:, sl], ca, sa, MLA_ROPE // 4) * MLA_SCALE).astype(BF16)
        ka_ref[0, i] = (k_full[:, sl] + kpe).astype(BF16)
        va_ref[0, i] = transposed(v_full[:, sl])

    for i in range(NA_HEADS):
        qn_ref[0, i] = (head(P_QN, i) * NA_SCALE).astype(BF16)
        kn_ref[0, i] = head(P_KN, i).astype(BF16)
        vn_ref[0, i] = with_ones(head(P_VN, i))

    cd, sd = cd_ref[...], sd_ref[...]
    for i in range(DIFF_HEADS):
        zq = _rope(head(P_QD, i), cd, sd, DIFF_QK // 4) * DIFF_SCALE
        qd_ref[0, 2 * i] = jnp.where(lane < DIFF_QK, zq, 0.0).astype(BF16)
        qd_ref[0, 2 * i + 1] = jnp.where(lane >= DIFF_QK, zq, 0.0).astype(BF16)
        kd_ref[0, i] = _rope(head(P_KD, i), cd, sd, DIFF_QK // 4).astype(BF16)
        vd_ref[0, i] = transposed(head(P_VD, i))

    cg, sg = cg_ref[...], sg_ref[...]
    gqn, gkn = gqn_ref[...], gkn_ref[...]
    for i in range(GQA_HEADS):
        qn = _rms(head(P_QG, i), gqn, GQA_DIM)
        qg_ref[0, i] = (_rope(qn, cg, sg, GQA_DIM // 4) * GQA_SCALE).astype(BF16)
    for i in range(GQA_KV_HEADS):
        kn = _rms(head(P_KG, i), gkn, GQA_DIM)
        kg_ref[0, i] = _rope(kn, cg, sg, GQA_DIM // 4).astype(BF16)
        vg_ref[0, i] = transposed(head(P_VG, i))


def _proj_call(xl, xc, mod_l, mod_c, wsm, gqa, wqup, gkva, wkvk, wkvv, gqn, gkn, tabs, layer):
    B, S, D = xl.shape
    tm = TOK_TILE
    n_lat = S // tm
    T = S + CTX_LEN
    grid = (n_lat + 1, B)
    tab_spec = pl.BlockSpec((tm, LANES), lambda t, b: (t, 0))
    in_specs = [
        pl.BlockSpec((1, tm, D), lambda t, b: (b, jnp.minimum(t, n_lat - 1), 0)),
        pl.BlockSpec((1, tm, D), lambda t, b: (b, 0, 0)),
        pl.BlockSpec((1, 1, 6 * D), lambda t, b: (b, 0, 0)),
        pl.BlockSpec((1, 1, 6 * D), lambda t, b: (0, 0, 0)),
        _layer_spec(wsm, layer), _const_spec(gqa.shape), _layer_spec(wqup, layer), _const_spec(gkva.shape),
        _layer_spec(wkvk, layer), _layer_spec(wkvv, layer), _const_spec(gqn.shape), _const_spec(gkn.shape),
    ] + [tab_spec] * 6

    def o(n_heads, width):
        return (jax.ShapeDtypeStruct((B, n_heads, T, width), BF16),
                pl.BlockSpec((1, n_heads, tm, width), lambda t, b: (b, 0, t, 0)))

    def ot(n_heads):
        return (jax.ShapeDtypeStruct((B, n_heads, VT_ROWS, T), BF16),
                pl.BlockSpec((1, n_heads, VT_ROWS, tm), lambda t, b: (b, 0, 0, t)))

    outs = [o(MLA_HEADS, HEAD_PAD), o(MLA_HEADS, HEAD_PAD), ot(MLA_HEADS),
            o(NA_HEADS, HEAD_PAD), o(NA_HEADS, HEAD_PAD), o(NA_HEADS, V_PAD),
            o(2 * DIFF_HEADS, HEAD_PAD), o(DIFF_HEADS, HEAD_PAD), ot(DIFF_HEADS),
            o(GQA_HEADS, HEAD_PAD), o(GQA_KV_HEADS, HEAD_PAD), ot(GQA_KV_HEADS)]
    return pl.pallas_call(
        functools.partial(_proj_kernel, n_lat_tiles=n_lat),
        grid=grid,
        in_specs=in_specs,
        out_specs=[s for _, s in outs],
        out_shape=[s for s, _ in outs],
        compiler_params=_cparams(("arbitrary", "arbitrary")),
        name="in_proj",
    )(xl, xc, mod_l, mod_c, wsm, gqa, wqup, gkva, wkvk, wkvv, gqn, gkn, *tabs)


def _key_chunks(t_keys):
    n_tiles = t_keys // KEY_TILE
    n_chunks = min(KEY_CHUNKS, n_tiles)
    sizes = [n_tiles // n_chunks + (1 if c < n_tiles % n_chunks else 0) for c in range(n_chunks)]
    bounds = np.cumsum([0] + sizes) * KEY_TILE
    return [(int(bounds[c]), int(bounds[c + 1])) for c in range(n_chunks)]


def _attn_kernel(*refs, n_kv, group, diff, lam_init):
    q_ref, k_ref, v_ref = refs[:3]
    s_ref, ma_ref, mb_ref, pa_ref, pb_ref = refs[-5:]
    step = pl.program_id(0)
    tq = q_ref.shape[2]
    n_dots = s_ref.shape[0]
    cols = s_ref.shape[2]
    split = n_dots // n_kv

    @pl.when(step == 0)
    def _():
        s_ref[...] = jnp.zeros(s_ref.shape, F32)
        mb_ref[...] = jnp.zeros(mb_ref.shape, F32)
        pb_ref[...] = jnp.ones(pb_ref.shape, BF16)

    def body(m_write, m_read, p_write, p_read):
        qs = []
        for j in range(n_kv):
            q = q_ref[0, j * group:(j + 1) * group].reshape(group * tq, HEAD_PAD)
            qs += [q[r * cols:(r + 1) * cols] for r in range(split)]
        acc = [None] * n_dots
        run_max = [None] * n_dots
        for lo, hi in _key_chunks(k_ref.shape[2]):
            for d in range(n_dots):
                j = d // split
                part = _bdot(v_ref[0, j, :, lo:hi], p_read[d, lo:hi, :])
                acc[d] = part if acc[d] is None else acc[d] + part
                p_write[d, lo:hi, :] = jnp.exp2(s_ref[d, lo:hi, :] - m_read[d]).astype(BF16)
                s = lax.dot_general(k_ref[0, j, lo:hi, :], qs[d], NT_DIMS, preferred_element_type=F32)
                s_ref[d, lo:hi, :] = s
                pm = jnp.max(s.reshape(MAX_CHAINS, (hi - lo) // MAX_CHAINS, cols), axis=1)
                run_max[d] = pm if run_max[d] is None else jnp.maximum(run_max[d], pm)
        heads = []
        for j in range(n_kv):
            outs = []
            for d in range(j * split, (j + 1) * split):
                m_write[d] = jnp.max(run_max[d], axis=0, keepdims=True)
                outs.append(acc[d][:V_DIM] / acc[d][V_DIM:V_DIM + 1])
            on = outs[0] if split == 1 else jnp.concatenate(outs, axis=1)
            heads += [on[:, g * tq:(g + 1) * tq] for g in range(group)]
        _attn_finish(heads, refs, diff, lam_init)

    parity = lax.rem(step, 2)

    @pl.when(parity == 0)
    def _():
        body(ma_ref, mb_ref, pa_ref, pb_ref)

    @pl.when(parity == 1)
    def _():
        body(mb_ref, ma_ref, pb_ref, pa_ref)


def _attn_finish(heads, refs, diff, lam_init):
    if diff:
        lq1_ref, lk1_ref, lq2_ref, lk2_ref, gsub_ref, o_ref = refs[3:9]
    else:
        o_ref = refs[3]
    if diff:
        ys = []
        for j in range(2):
            dot_l = lambda a, b: jnp.sum(a[0, j:j + 1] * b[0, j:j + 1], axis=-1, keepdims=True)
            lam = jnp.exp(dot_l(lq1_ref, lk1_ref)) - jnp.exp(dot_l(lq2_ref, lk2_ref)) + lam_init
            y = heads[2 * j] - lam * heads[2 * j + 1]
            ms = jnp.mean(y * y, axis=0, keepdims=True)
            ys.append(y * lax.rsqrt(ms + EPS) * gsub_ref[...] * (1.0 - lam_init))
        heads = ys
    o_ref[0] = jnp.concatenate(heads, axis=0).T.astype(BF16)


def _transpose_v(v):
    B, H, T, _ = v.shape
    vt = jnp.swapaxes(v[..., :V_DIM], -1, -2)
    tail = jnp.zeros((B, H, VT_ROWS - V_DIM, T), v.dtype).at[:, :, 0, :].set(1.0)
    return jnp.concatenate([vt, tail], axis=2)


def _attn_call(q, k, vt, *, group, n_kv, tq, q_blk0, n_q, t_blk, k_blk0, diff_args=None, lam_init=0.0, name):
    B, Hq = q.shape[0], q.shape[1]
    n_pairs = Hq // (n_kv * group)
    qh = n_kv * group
    n_units = B * n_pairs * n_q
    n_dots = qh * tq // SCORE_COLS

    def unit(u):
        return u // (n_pairs * n_q), (u // n_q) % n_pairs, u % n_q

    def stage1(s):
        return unit(jnp.minimum(s, n_units - 1))

    def stage3(s):
        return unit(jnp.maximum(s - 2, 0))

    def q_map(s):
        b, p, i = stage1(s)
        return b, p, q_blk0 + i, 0

    def k_map(s):
        b, p, _ = stage1(s)
        return b, p, k_blk0, 0

    def v_map(s):
        b, p, _ = stage3(s)
        return b, p, 0, k_blk0

    def o_map(s):
        b, p, i = stage3(s)
        return b, i, p

    in_specs = [
        pl.BlockSpec((1, qh, tq, HEAD_PAD), q_map),
        pl.BlockSpec((1, n_kv, t_blk, HEAD_PAD), k_map),
        pl.BlockSpec((1, n_kv, VT_ROWS, t_blk), v_map),
    ]
    args = [q, k, vt]
    if diff_args is not None:
        lam_spec = pl.BlockSpec((1, 2, DIFF_QK), lambda s: (stage3(s)[1], 0, 0))
        in_specs += [lam_spec] * 4 + [pl.BlockSpec((V_DIM, 1), lambda s: (0, 0))]
        args += list(diff_args)
    score_buf = pltpu.VMEM((n_dots, t_blk, SCORE_COLS), F32)
    prob_buf = pltpu.VMEM((n_dots, t_blk, SCORE_COLS), BF16)
    max_buf = pltpu.VMEM((n_dots, 1, SCORE_COLS), F32)
    return pl.pallas_call(
        functools.partial(_attn_kernel, n_kv=n_kv, group=group, diff=diff_args is not None, lam_init=lam_init),
        grid=(n_units + 2,),
        in_specs=in_specs,
        out_specs=pl.BlockSpec((1, tq, LANES), o_map),
        out_shape=jax.ShapeDtypeStruct((B, n_q * tq, n_pairs * LANES), BF16),
        scratch_shapes=[score_buf, max_buf, max_buf, prob_buf, prob_buf],
        compiler_params=_cparams(("arbitrary",)),
        name=name,
    )(*args)


def _na_kernel(q_ref, k_ref, v_ref, *rest, n_lat, win):
    bias_refs, o_ref = rest[:-1], rest[-1]
    rows = n_lat // GRID_W
    tq = NA_GROUP_ROWS * GRID_W
    lane = lax.broadcasted_iota(jnp.int32, (tq, LANES), 1)
    for i, bias_ref in enumerate(bias_refs):
        g = pl.program_id(1) * len(bias_refs) + i
        start = jnp.clip(NA_GROUP_ROWS * g - NA_ROWS // 2, 0, rows - NA_WIN_ROWS) * GRID_W
        start = pl.multiple_of(start, GRID_W)
        outs = []
        for h in range(NA_HEADS):
            q = q_ref[0, h, i * tq:(i + 1) * tq, :]
            s_l = lax.dot_general(q, k_ref[0, h, pl.ds(start, win), :], NT_DIMS, preferred_element_type=F32)
            s_l = s_l + bias_ref[0, h]
            s_c = lax.dot_general(q, k_ref[0, h, n_lat:, :], NT_DIMS, preferred_element_type=F32)
            m = jnp.maximum(jnp.max(s_l, axis=-1, keepdims=True), jnp.max(s_c, axis=-1, keepdims=True))
            p_l = jnp.exp2(s_l - m).astype(BF16)
            p_c = jnp.exp2(s_c - m).astype(BF16)
            oe = _bdot(p_l, v_ref[0, h, pl.ds(start, win), :]) + _bdot(p_c, v_ref[0, h, n_lat:, :])
            outs.append(oe[:, :LANES] / oe[:, LANES:])
        out = jnp.concatenate([jnp.where(lane < 64, outs[0], outs[1]), jnp.where(lane < 64, outs[2], outs[3])], axis=-1)
        o_ref[0, i * tq:(i + 1) * tq, :] = out.astype(BF16)


def _na_call(q, k, v, bias, n_lat, layer):
    B, H, T, _ = q.shape
    tq = NA_GROUP_ROWS * GRID_W
    win = NA_WIN_ROWS * GRID_W
    n_groups = n_lat // tq
    per_step = NA_GROUPS_PER_STEP

    def bias_spec(i):
        def variant(b, s):
            g = s * per_step + i
            return 3 * layer + jnp.where(g == 0, 0, jnp.where(g == n_groups - 1, 2, 1)), 0, 0, 0
        return pl.BlockSpec((1, H, tq, win), variant)

    return pl.pallas_call(
        functools.partial(_na_kernel, n_lat=n_lat, win=win),
        grid=(B, n_groups // per_step),
        in_specs=[
            pl.BlockSpec((1, H, per_step * tq, HEAD_PAD), lambda b, s: (b, 0, s, 0)),
            pl.BlockSpec((1, H, T, HEAD_PAD), lambda b, s: (b, 0, 0, 0)),
            pl.BlockSpec((1, H, T, V_PAD), lambda b, s: (b, 0, 0, 0)),
        ] + [bias_spec(i) for i in range(per_step)],
        out_specs=pl.BlockSpec((1, per_step * tq, H * 64), lambda b, s: (b, s, 0)),
        out_shape=jax.ShapeDtypeStruct((B, n_lat, H * 64), BF16),
        compiler_params=_cparams(("parallel", "arbitrary")),
        name="na_attn",
    )(q, k, v, *([bias] * per_step))


def _merge_kernel(x_ref, mod_ref, ya_ref, yb_ref, yc_ref, yd_ref, wg_ref, wb_ref, wo_ref, lg_ref, lb_ref, o_ref):
    D = D_MODEL
    mod = mod_ref[0]
    sh, sc, gate_res = mod[:, 0:D], mod[:, D:2 * D], mod[:, 2 * D:3 * D]
    for rows in _row_streams(x_ref.shape[1]):
        x = x_ref[0, rows]
        h = (x * (1.0 + sc) + sh).astype(BF16)
        acc = None
        for i, y_ref in enumerate((ya_ref, yb_ref, yc_ref, yd_ref)):
            gate = _sigmoid(_bdot(h, wg_ref[0, :, i * D:(i + 1) * D]))
            term = gate * _bdot(y_ref[0, rows], wb_ref[0, i])
            acc = term if acc is None else acc + term
        out = _bdot(acc.astype(BF16), wo_ref[0])
        o_ref[0, rows] = _layer_norm(DEEPNORM_ALPHA * x + gate_res * out, lg_ref[...], lb_ref[...])


def _mix_ffn_kernel(x_ref, mod_ref, ya_ref, yb_ref, yc_ref, yd_ref, wg_ref, wb_ref, wo_ref, l1g_ref, l1b_ref,
                    wgu_ref, wd_ref, l2g_ref, l2b_ref, o_ref):
    D = D_MODEL
    mod = mod_ref[0]
    sh1, sc1, g1 = mod[:, 0:D], mod[:, D:2 * D], mod[:, 2 * D:3 * D]
    sh2, sc2, g2 = mod[:, 3 * D:4 * D], mod[:, 4 * D:5 * D], mod[:, 5 * D:6 * D]
    for rows in _row_streams(x_ref.shape[1]):
        x = x_ref[0, rows]
        h = (x * (1.0 + sc1) + sh1).astype(BF16)
        acc = None
        for i, y_ref in enumerate((ya_ref, yb_ref, yc_ref, yd_ref)):
            gate = _sigmoid(_bdot(h, wg_ref[0, :, i * D:(i + 1) * D]))
            term = gate * _bdot(y_ref[0, rows], wb_ref[0, i])
            acc = term if acc is None else acc + term
        out = _bdot(acc.astype(BF16), wo_ref[0])
        x1 = _layer_norm(DEEPNORM_ALPHA * x + g1 * out, l1g_ref[...], l1b_ref[...])
        h2 = (x1 * (1.0 + sc2) + sh2).astype(BF16)
        gu = _bdot(h2, wgu_ref[0])
        g, u = gu[:, :D_FF], gu[:, D_FF:]
        act = (g * _sigmoid(g) * u).astype(BF16)
        down = _bdot(act, wd_ref[0])
        o_ref[0, rows] = _layer_norm(DEEPNORM_ALPHA * x1 + g2 * down, l2g_ref[...], l2b_ref[...])


def _mix_ffn_call(x, mod, ys, wg, wb, wo, l1g, l1b, wgu, wd, l2g, l2b, tm, layer):
    B, S, D = x.shape
    tok = lambda w: pl.BlockSpec((1, tm, w), lambda b, i: (b, i, 0))
    return pl.pallas_call(
        _mix_ffn_kernel,
        grid=(B, S // tm),
        in_specs=[tok(D), pl.BlockSpec((1, 1, 6 * D), lambda b, i: (b, 0, 0))] + [tok(BRANCH_W)] * 4
        + [_layer_spec(wg, layer), _layer_spec(wb, layer), _layer_spec(wo, layer), _const_spec(l1g.shape),
           _const_spec(l1b.shape), _layer_spec(wgu, layer), _layer_spec(wd, layer), _const_spec(l2g.shape),
           _const_spec(l2b.shape)],
        out_specs=tok(D),
        out_shape=jax.ShapeDtypeStruct((B, S, D), F32),
        compiler_params=_cparams(("parallel", "arbitrary")),
        name="mix_ffn",
    )(x, mod, *ys, wg, wb, wo, l1g, l1b, wgu, wd, l2g, l2b)


def _merge_call(x, mod, ys, wg, wb, wo, lg, lb, tm, layer):
    B, S, D = x.shape
    per_batch_mod = mod.shape[0] > 1
    tok = lambda w: pl.BlockSpec((1, tm, w), lambda b, i: (b, i, 0))
    return pl.pallas_call(
        _merge_kernel,
        grid=(B, S // tm),
        in_specs=[tok(D), pl.BlockSpec((1, 1, 6 * D), lambda b, i: (b if per_batch_mod else 0, 0, 0))]
        + [tok(BRANCH_W)] * 4
        + [_layer_spec(wg, layer), _layer_spec(wb, layer), _layer_spec(wo, layer), _const_spec(lg.shape), _const_spec(lb.shape)],
        out_specs=tok(D),
        out_shape=jax.ShapeDtypeStruct((B, S, D), F32),
        compiler_params=_cparams(("parallel", "arbitrary")),
        name="merge_ln",
    )(x, mod, *ys, wg, wb, wo, lg, lb)


def _ffn_kernel(x_ref, mod_ref, wgu_ref, wd_ref, lg_ref, lb_ref, o_ref):
    D = D_MODEL
    mod = mod_ref[0]
    sh, sc, gate_res = mod[:, 3 * D:4 * D], mod[:, 4 * D:5 * D], mod[:, 5 * D:6 * D]
    for rows in _row_streams(x_ref.shape[1]):
        x = x_ref[0, rows]
        h = (x * (1.0 + sc) + sh).astype(BF16)
        gu = _bdot(h, wgu_ref[0])
        g, u = gu[:, :D_FF], gu[:, D_FF:]
        act = (g * _sigmoid(g) * u).astype(BF16)
        down = _bdot(act, wd_ref[0])
        o_ref[0, rows] = _layer_norm(DEEPNORM_ALPHA * x + gate_res * down, lg_ref[...], lb_ref[...])


def _ffn_call(x, mod, wgu, wd, lg, lb, tm, layer):
    B, S, D = x.shape
    per_batch_mod = mod.shape[0] > 1
    tok = pl.BlockSpec((1, tm, D), lambda b, i: (b, i, 0))
    return pl.pallas_call(
        _ffn_kernel,
        grid=(B, S // tm),
        in_specs=[tok, pl.BlockSpec((1, 1, 6 * D), lambda b, i: (b if per_batch_mod else 0, 0, 0)),
                  _layer_spec(wgu, layer), _layer_spec(wd, layer), _const_spec(lg.shape), _const_spec(lb.shape)],
        out_specs=tok,
        out_shape=jax.ShapeDtypeStruct((B, S, D), F32),
        compiler_params=_cparams(("parallel", "arbitrary")),
        name="ffn_ln",
    )(x, mod, wgu, wd, lg, lb)


def kernel(x, c, ctx, c_ctx, w_ada, b_ada, w_in, g_q_a, w_q_up, g_kv_a, w_kv_up, rpb, lam_q1, lam_k1, lam_q2, lam_k2,
           g_sub, g_qn, g_kn, w_branch, w_out, ln1_g, ln1_b, w_gate_up, w_down, ln2_g, ln2_b):
    B, S, D = x.shape
    rows = S // GRID_W
    n_q = S // Q_TILE
    T = S + CTX_LEN
    ctx_blk = S // CTX_LEN

    c8 = jnp.concatenate([c, c_ctx[None, :], jnp.zeros((8 - B - 1, D), F32)], axis=0)
    mod = _ada_call(c8, w_ada, b_ada)
    tabs = _rope_tables(S)
    w_in_bf = w_in.astype(BF16)
    wsm = _small_weights(w_in_bf)
    wg = w_in_bf[..., GATE_OFF:]
    wqup = _pad_heads(w_q_up, MLA_HEADS, MLA_NOPE + MLA_ROPE).astype(BF16)
    wkvk, wkvv = _kv_up_weights(w_kv_up)
    wb, wo = w_branch.astype(BF16), w_out.astype(BF16)
    wgu, wd = w_gate_up.astype(BF16), w_down.astype(BF16)
    bias = _na_bias_tables(rpb, rows)
    row = lambda v: v.reshape(1, -1)
    pad64 = lambda v: jnp.concatenate([v, jnp.zeros_like(v)]).reshape(1, LANES)

    xl, xc = x, ctx
    for l in range(DEPTH):
        need_ctx = l < DEPTH - 1
        lam_init = 0.8 - 0.6 * math.exp(-0.3 * l)
        mod_l = mod[l, :B].reshape(B, 1, 6 * D)
        mod_c = mod[l, B:B + 1].reshape(1, 1, 6 * D)
        (qa, ka, vat, qn, kn, vn, qd, kd, vdt, qg, kg, vgt) = _proj_call(
            xl, xc, mod_l, mod_c, wsm, row(g_q_a[l]), wqup, row(g_kv_a[l]), wkvk, wkvv,
            pad64(g_qn[l]), pad64(g_kn[l]), tabs, layer=l)

        diff_args = (lam_q1[l].reshape(2, 2, DIFF_QK), lam_k1[l].reshape(2, 2, DIFF_QK),
                     lam_q2[l].reshape(2, 2, DIFF_QK), lam_k2[l].reshape(2, 2, DIFF_QK),
                     g_sub[l].reshape(V_DIM, 1))

        lat = lambda tq: dict(tq=tq, q_blk0=0, n_q=S // tq, t_blk=T, k_blk0=0)
        ya = _attn_call(qa, ka, vat, group=1, n_kv=2, name="mla_attn", **lat(ATTN_TILE))
        yb = _na_call(qn, kn, vn, bias, S, layer=l)
        yc = _attn_call(qd, kd, vdt, group=2, n_kv=2, diff_args=diff_args, lam_init=lam_init, name="diff_attn",
                        **lat(ATTN_TILE // 2))
        yd = _attn_call(qg, kg, vgt, group=2, n_kv=1, name="gqa_attn", **lat(ATTN_TILE))

        ln1 = (row(ln1_g[l]), row(ln1_b[l]))
        ln2 = (row(ln2_g[l]), row(ln2_b[l]))

        if need_ctx:
            cx = dict(tq=CTX_LEN, q_blk0=ctx_blk, n_q=1, t_blk=CTX_LEN, k_blk0=ctx_blk)
            ya_c = _attn_call(qa, ka, vat, group=1, n_kv=2, name="mla_attn_ctx", **cx)
            ctx0 = dict(tq=CTX_LEN, q_blk0=0, n_q=1, t_blk=CTX_LEN, k_blk0=0)
            yb_c = _attn_call(qn[:, :, S:], kn[:, :, S:], _transpose_v(vn[:, :, S:]), group=1, n_kv=2,
                              name="na_attn_ctx", **ctx0)
            yc_c = _attn_call(qd, kd, vdt, group=2, n_kv=2, diff_args=diff_args, lam_init=lam_init,
                              name="diff_attn_ctx", **cx)
            yd_c = _attn_call(qg, kg, vgt, group=2, n_kv=1, name="gqa_attn_ctx", **cx)
            xc1 = _merge_call(xc, mod_c, (ya_c, yb_c, yc_c, yd_c), wg, wb, wo, *ln1, tm=CTX_LEN, layer=l)
            xc = _ffn_call(xc1, mod_c, wgu, wd, *ln2, tm=CTX_LEN, layer=l)

        xl = _mix_ffn_call(xl, mod_l, (ya, yb, yc, yd), wg, wb, wo, *ln1, wgu, wd, *ln2, tm=FUSED_TILE, layer=l)
    return xl
```
